```python
import jax, jax.numpy as jnp
from jax import lax
import numpy as np

D_MODEL = 1024
BATCH = 8
SEQ = 8192
DEPTH = 4

HEAD_DIM = 64
EPS = 1e-6
A_Q_HEADS = 12
A_KV_HEADS = 2
A_GROUP = A_Q_HEADS // A_KV_HEADS
WINDOW = 128
A_Q_W = A_Q_HEADS * HEAD_DIM
A_KV_W = A_KV_HEADS * HEAD_DIM
B_GROUPS = 6
B_GROUP_DIM = 128
B_WIDTH = B_GROUPS * B_GROUP_DIM
CHUNK = 128
N_MEM = 256
MEM_HEADS = 4
MEM_WIDTH = MEM_HEADS * HEAD_DIM
MIX_WIDTH = A_Q_W + MEM_WIDTH
A_IN = A_Q_W + 2 * A_KV_W + MEM_WIDTH
B_IN = 2 * B_WIDTH + MEM_WIDTH
D_FF = -(-8 * D_MODEL // (3 * 256)) * 256
N_A = (DEPTH + 1) // 2
N_B = DEPTH // 2

kernel_name = "hybrid_swa_sink_gmlp_memxattn_trunk"


def rms_norm(x, g):
    xf = x.astype(jnp.float32)
    y = xf * lax.rsqrt(jnp.mean(xf * xf, axis=-1, keepdims=True) + EPS)
    return (y * g.astype(jnp.float32)).astype(x.dtype)


def sliding_window_attention(q, k, v, sinks):
    b, s, _, hd = q.shape
    nb = s // WINDOW
    qb = q.reshape(b, nb, WINDOW, A_KV_HEADS, A_GROUP, hd)

    def with_prev(t):
        tb = t.reshape(b, nb, WINDOW, A_KV_HEADS, hd)
        prev = jnp.pad(tb[:, :-1], ((0, 0), (1, 0), (0, 0), (0, 0), (0, 0)))
        return jnp.concatenate([prev, tb], axis=2)

    kb, vb = with_prev(k), with_prev(v)
    scores = jnp.einsum('bnqhgd,bnkhd->bnhgqk', qb, kb).astype(jnp.float32) * (hd ** -0.5)
    qi = jnp.arange(WINDOW)[:, None]
    kj = jnp.arange(2 * WINDOW)[None, :]
    rel = qi + WINDOW - kj
    band = (rel >= 0) & (rel < WINDOW)
    not_pad = (jnp.arange(nb)[:, None, None] > 0) | (kj[None] >= WINDOW)
    mask = band[None] & not_pad
    scores = jnp.where(mask[None, :, None, None], scores, -jnp.inf)
    sink = sinks.astype(jnp.float32).reshape(A_KV_HEADS, A_GROUP)[None, None, :, :, None, None]
    m = jnp.maximum(jnp.max(scores, axis=-1, keepdims=True), sink)
    e = jnp.exp(scores - m)
    p = e / (jnp.sum(e, axis=-1, keepdims=True) + jnp.exp(sink - m))
    o = jnp.einsum('bnhgqk,bnkhd->bnqhgd', p.astype(v.dtype), vb)
    return o.reshape(b, s, A_Q_HEADS * hd)


def chunked_spatial_gating(z, w_s, b_s, ln_g, ln_b):
    b, s, _ = z.shape
    u, v = jnp.split(z, 2, axis=-1)
    v = v.reshape(b, s // CHUNK, CHUNK, B_GROUPS, B_GROUP_DIM)
    vf = v.astype(jnp.float32)
    mu = jnp.mean(vf, axis=-1, keepdims=True)
    var = jnp.mean(jnp.square(vf - mu), axis=-1, keepdims=True)
    vn = ((vf - mu) * lax.rsqrt(var + EPS) * ln_g.astype(jnp.float32) + ln_b.astype(jnp.float32)).astype(z.dtype)
    causal = jnp.tril(jnp.ones((CHUNK, CHUNK), dtype=bool))
    w = jnp.where(causal[None], w_s, jnp.zeros_like(w_s)).astype(vn.dtype)
    sv = jnp.einsum('gts,bnsgd->bntgd', w, vn) + b_s.T.astype(vn.dtype)[None, None, :, :, None]
    return u * sv.reshape(b, s, B_WIDTH).astype(z.dtype)


def memory_attention(q_mem, mem_n, w_kv):
    b, nm, _ = mem_n.shape
    kv = mem_n @ w_kv
    k, v = jnp.split(kv, 2, axis=-1)
    k = k.reshape(b, nm, MEM_HEADS, HEAD_DIM)
    v = v.reshape(b, nm, MEM_HEADS, HEAD_DIM)
    s = jnp.einsum('bshd,bmhd->bhsm', q_mem, k).astype(jnp.float32) * (HEAD_DIM ** -0.5)
    p = jax.nn.softmax(s, axis=-1).astype(v.dtype)
    o = jnp.einsum('bhsm,bmhd->bshd', p, v)
    return o.reshape(q_mem.shape[0], q_mem.shape[1], MEM_WIDTH)


def setup_inputs(seed: int = 0) -> dict:
    key = jax.random.key(seed)
    ks = jax.random.split(key, 20)
    f32 = jnp.float32

    def nrm(k, shape, fan_in):
        return jax.random.normal(k, shape, f32) * (fan_in ** -0.5)

    def gain(k, shape):
        return 1.0 + 0.05 * jax.random.normal(k, shape, f32)

    return {
        "x": jax.random.normal(ks[0], (BATCH, SEQ, D_MODEL), f32),
        "mem": jax.random.normal(ks[1], (BATCH, N_MEM, D_MODEL), f32),
        "mem_norm_g": gain(ks[2], (D_MODEL,)),
        "mix_norm_g": gain(ks[3], (DEPTH, D_MODEL)),
        "ffn_norm_g": gain(ks[4], (DEPTH, D_MODEL)),
        "final_norm_g": gain(ks[5], (D_MODEL,)),
        "a_w_in": nrm(ks[6], (N_A, D_MODEL, A_IN), D_MODEL),
        "a_sinks": 0.5 * jax.random.normal(ks[7], (N_A, A_Q_HEADS), f32),
        "a_w_out": nrm(ks[8], (N_A, MIX_WIDTH, D_MODEL), MIX_WIDTH),
        "b_w_in": nrm(ks[9], (N_B, D_MODEL, B_IN), D_MODEL),
        "b_w_s": nrm(ks[10], (N_B, B_GROUPS, CHUNK, CHUNK), CHUNK),
        "b_bias_s": 1.0 + 0.05 * jax.random.normal(ks[11], (N_B, B_GROUPS, CHUNK), f32),
        "b_ln_g": gain(ks[12], (N_B, B_GROUPS, B_GROUP_DIM)),
        "b_ln_b": 0.02 * jax.random.normal(ks[13], (N_B, B_GROUPS, B_GROUP_DIM), f32),
        "b_w_out": nrm(ks[14], (N_B, MIX_WIDTH, D_MODEL), MIX_WIDTH),
        "w_mem_kv": nrm(ks[15], (DEPTH, D_MODEL, 2 * MEM_WIDTH), D_MODEL),
        "w_gate_up": nrm(ks[16], (DEPTH, D_MODEL, 2 * D_FF), D_MODEL),
        "w_down": nrm(ks[17], (DEPTH, D_FF, D_MODEL), D_FF),
    }


def reference(x, mem, mem_norm_g, mix_norm_g, ffn_norm_g, final_norm_g,
              a_w_in, a_sinks, a_w_out,
              b_w_in, b_w_s, b_bias_s, b_ln_g, b_ln_b, b_w_out,
              w_mem_kv, w_gate_up, w_down):
    b, s, _ = x.shape
    mem_n = rms_norm(mem, mem_norm_g)
    h = x
    for i in range(DEPTH):
        j = i // 2
        xn = rms_norm(h, mix_norm_g[i])
        if i % 2 == 0:
            proj = xn @ a_w_in[j]
            q, k, v, q_mem = jnp.split(proj, [A_Q_W, A_Q_W + A_KV_W, A_Q_W + 2 * A_KV_W], axis=-1)
            mix = sliding_window_attention(
                q.reshape(b, s, A_Q_HEADS, HEAD_DIM),
                k.reshape(b, s, A_KV_HEADS, HEAD_DIM),
                v.reshape(b, s, A_KV_HEADS, HEAD_DIM),
                a_sinks[j])
            w_out = a_w_out[j]
        else:
            proj = xn @ b_w_in[j]
            z, q_mem = jnp.split(proj, [2 * B_WIDTH], axis=-1)
            mix = chunked_spatial_gating(jax.nn.gelu(z), b_w_s[j], b_bias_s[j], b_ln_g[j], b_ln_b[j])
            w_out = b_w_out[j]
        mem_out = memory_attention(q_mem.reshape(b, s, MEM_HEADS, HEAD_DIM), mem_n, w_mem_kv[i])
        h = h + jnp.concatenate([mix, mem_out.astype(mix.dtype)], axis=-1) @ w_out
        hn = rms_norm(h, ffn_norm_g[i])
        gate, up = jnp.split(hn @ w_gate_up[i], 2, axis=-1)
        h = h + (jax.nn.silu(gate) * up) @ w_down[i]
    return rms_norm(h, final_norm_g)
```

```python
import functools

import jax
import jax.numpy as jnp
from jax import lax
from jax.experimental import pallas as pl
from jax.experimental.pallas import tpu as pltpu

D_MODEL = 1024
DEPTH = 4
HEAD_DIM = 64
EPS = 1e-6
A_Q_HEADS = 12
A_KV_HEADS = 2
A_GROUP = A_Q_HEADS // A_KV_HEADS
WINDOW = 128
A_Q_W = A_Q_HEADS * HEAD_DIM
A_KV_W = A_KV_HEADS * HEAD_DIM
B_GROUPS = 6
B_GROUP_DIM = 128
B_WIDTH = B_GROUPS * B_GROUP_DIM
CHUNK = 128
N_MEM = 256
MEM_HEADS = 4
MEM_WIDTH = MEM_HEADS * HEAD_DIM
MIX_WIDTH = A_Q_W + MEM_WIDTH
A_IN = A_Q_W + 2 * A_KV_W + MEM_WIDTH
B_IN = 2 * B_WIDTH + MEM_WIDTH
D_FF = 2816

F32 = jnp.float32
BF16 = jnp.bfloat16

LANES = 128
PAIR_W = 2 * HEAD_DIM
MEM_PAIRS = MEM_HEADS // 2
A_PAIRS_PER_KV = A_GROUP // 2
SCALE = HEAD_DIM ** -0.5
FF_TILE = 256
N_FF_TILES = D_FF // FF_TILE

MIX_ROWS = 512
FFN_ROWS = 512
VMEM_LIMIT = 56 * 1024 * 1024

NT_DIMS = (((1,), (1,)), ((), ()))


def _rms(x, g):
    ms = jnp.mean(x * x, axis=-1, keepdims=True)
    return x * lax.rsqrt(ms + EPS) * g


def _lo_lanes(shape):
    return lax.broadcasted_iota(jnp.int32, shape, len(shape) - 1) < HEAD_DIM


def _resident(shape):
    nd = len(shape)
    return pl.BlockSpec(shape, lambda *_: (0,) * nd, pipeline_mode=pl.Buffered(1))


def _memkv_body(mem_ref, g_ref, w_ref, kk_ref, vv_ref):
    mem_n = _rms(mem_ref[...], g_ref[...]).astype(BF16)
    kv = jnp.dot(mem_n, w_ref[...], preferred_element_type=F32)
    lo = _lo_lanes((N_MEM, PAIR_W))
    for p in range(MEM_PAIRS):
        k = kv[:, p * PAIR_W:(p + 1) * PAIR_W]
        v = kv[:, MEM_WIDTH + p * PAIR_W:MEM_WIDTH + (p + 1) * PAIR_W]
        kk_ref[p, 0:N_MEM, :] = jnp.where(lo, k, 0.0).astype(BF16)
        kk_ref[p, N_MEM:2 * N_MEM, :] = jnp.where(lo, 0.0, k).astype(BF16)
        vv_ref[p, 0:N_MEM, :] = jnp.where(lo, v, 0.0).astype(BF16)
        vv_ref[p, N_MEM:2 * N_MEM, :] = jnp.where(lo, 0.0, v).astype(BF16)


def _mem_kv(mem, mem_norm_g, w_mem_kv_bf16):
    batch = mem.shape[0]
    out_shape = jax.ShapeDtypeStruct((DEPTH, batch, MEM_PAIRS, 2 * N_MEM, PAIR_W), BF16)
    out_spec = pl.BlockSpec((None, None, MEM_PAIRS, 2 * N_MEM, PAIR_W), lambda i, b: (i, b, 0, 0, 0))
    return pl.pallas_call(
        _memkv_body,
        grid=(DEPTH, batch),
        in_specs=[
            pl.BlockSpec((None, N_MEM, D_MODEL), lambda i, b: (b, 0, 0)),
            pl.BlockSpec((1, D_MODEL), lambda i, b: (0, 0)),
            pl.BlockSpec((None, D_MODEL, 2 * MEM_WIDTH), lambda i, b: (i, 0, 0)),
        ],
        out_specs=[out_spec, out_spec],
        out_shape=[out_shape, out_shape],
        name="mem_kv",
        compiler_params=pltpu.CompilerParams(dimension_semantics=("arbitrary", "arbitrary")),
    )(mem, mem_norm_g.reshape(1, D_MODEL), w_mem_kv_bf16)


def _pair_softmax_pv(s, vv, maxes_with=None):
    nk = s.shape[1] // 2
    es, rs = [], []
    for c in range(2):
        sc = s[:, c * nk:(c + 1) * nk]
        m = jnp.max(sc, axis=-1, keepdims=True)
        if maxes_with is not None:
            m = jnp.maximum(m, maxes_with[c])
        e = jnp.exp(sc - m)
        d = jnp.sum(e, axis=-1, keepdims=True)
        if maxes_with is not None:
            d = d + jnp.exp(maxes_with[c] - m)
        es.append(e.astype(BF16))
        rs.append(1.0 / d)
    o = jnp.dot(jnp.concatenate(es, axis=1), vv, preferred_element_type=F32)
    return o * jnp.where(_lo_lanes(o.shape), rs[0], rs[1])


def _mem_attention(qm, kkm_ref, vvm_ref, mix_ref, rows, col0):
    for p in range(MEM_PAIRS):
        qp = qm[:, p * PAIR_W:(p + 1) * PAIR_W].astype(BF16)
        s = lax.dot_general(qp, kkm_ref[p], NT_DIMS, preferred_element_type=F32)
        o = _pair_softmax_pv(s, vvm_ref[p])
        mix_ref[rows, col0 + p * PAIR_W:col0 + (p + 1) * PAIR_W] = o.astype(BF16)


def _swa_body(h_ref, g_ref, win_ref, sink_ref, kkm_ref, vvm_ref, wout_ref, o_ref,
              proj_ref, mix_ref, kprev_ref, vprev_ref):
    s_idx = pl.program_id(1)
    n_blocks = h_ref.shape[0] // WINDOW

    @pl.when(s_idx == 0)
    def _():
        kprev_ref[...] = jnp.zeros_like(kprev_ref)
        vprev_ref[...] = jnp.zeros_like(vprev_ref)

    xn = _rms(h_ref[...], g_ref[...]).astype(BF16)
    proj_ref[...] = jnp.dot(xn, win_ref[...], preferred_element_type=F32)

    qi = lax.broadcasted_iota(jnp.int32, (WINDOW, 2 * WINDOW), 0)
    kj = lax.broadcasted_iota(jnp.int32, (WINDOW, 2 * WINDOW), 1)
    is_cur = kj >= WINDOW
    rel = qi + WINDOW - kj
    band = jnp.logical_and(rel >= 0, rel < WINDOW)
    lo = _lo_lanes((2 * WINDOW, PAIR_W))

    def block(n, carry):
        rows = pl.ds(pl.multiple_of(n * WINDOW, WINDOW), WINDOW)
        q = proj_ref[rows, 0:A_Q_W] * SCALE
        k = proj_ref[rows, A_Q_W:A_Q_W + A_KV_W]
        v = proj_ref[rows, A_Q_W + A_KV_W:A_Q_W + 2 * A_KV_W]
        kcat = jnp.concatenate([kprev_ref[...], k], axis=0)
        vcat = jnp.concatenate([vprev_ref[...], v], axis=0)
        kswap = pltpu.roll(kcat, HEAD_DIM, 1)
        vswap = pltpu.roll(vcat, HEAD_DIM, 1)
        has_prev = jnp.logical_or(s_idx > 0, n > 0)
        mask = jnp.logical_and(band, jnp.logical_or(is_cur, has_prev))
        for kvh in range(A_KV_HEADS):
            k_nat, k_swp = (kcat, kswap) if kvh == 0 else (kswap, kcat)
            v_nat, v_swp = (vcat, vswap) if kvh == 0 else (vswap, vcat)
            kk = jnp.concatenate([jnp.where(lo, k_nat, 0.0), jnp.where(lo, 0.0, k_swp)], axis=0)
            vv = jnp.concatenate([jnp.where(lo, v_nat, 0.0), jnp.where(lo, 0.0, v_swp)], axis=0)
            kk = kk.astype(BF16)
            vv = vv.astype(BF16)
            pair0 = kvh * A_PAIRS_PER_KV
            qs = jnp.concatenate(
                [q[:, (pair0 + j) * PAIR_W:(pair0 + j + 1) * PAIR_W] for j in range(A_PAIRS_PER_KV)],
                axis=0).astype(BF16)
            s = lax.dot_general(qs, kk, NT_DIMS, preferred_element_type=F32)
            for j in range(A_PAIRS_PER_KV):
                head0 = 2 * (pair0 + j)
                sj = s[j * WINDOW:(j + 1) * WINDOW, :]
                sj = jnp.concatenate(
                    [jnp.where(mask, sj[:, c * 2 * WINDOW:(c + 1) * 2 * WINDOW], -jnp.inf) for c in range(2)],
                    axis=1)
                o = _pair_softmax_pv(sj, vv, (sink_ref[head0], sink_ref[head0 + 1]))
                mix_ref[rows, (pair0 + j) * PAIR_W:(pair0 + j + 1) * PAIR_W] = o.astype(BF16)
        kprev_ref[...] = k
        vprev_ref[...] = v
        qm = proj_ref[rows, A_Q_W + 2 * A_KV_W:A_IN] * SCALE
        _mem_attention(qm, kkm_ref, vvm_ref, mix_ref, rows, A_Q_W)
        return carry

    lax.fori_loop(0, n_blocks, block, 0)
    o_ref[...] = h_ref[...] + jnp.dot(mix_ref[...], wout_ref[...], preferred_element_type=F32)


def _gelu_tanh(x):
    return 0.5 * x * (1.0 + jnp.tanh(0.7978845608028654 * (x + 0.044715 * (x * x * x))))


def _gmlp_body(h_ref, g_ref, win_ref, ws_ref, bst_ref, lng_ref, lnb_ref, kkm_ref, vvm_ref, wout_ref, o_ref,
               proj_ref, mix_ref, wtril_ref):
    n_blocks = h_ref.shape[0] // CHUNK
    ti = lax.broadcasted_iota(jnp.int32, (CHUNK, CHUNK), 0)
    si = lax.broadcasted_iota(jnp.int32, (CHUNK, CHUNK), 1)
    causal = si <= ti
    for grp in range(B_GROUPS):
        wtril_ref[grp] = jnp.where(causal, ws_ref[grp], 0.0).astype(BF16)

    xn = _rms(h_ref[...], g_ref[...]).astype(BF16)
    proj_ref[...] = jnp.dot(xn, win_ref[...], preferred_element_type=F32)

    def block(n, carry):
        rows = pl.ds(pl.multiple_of(n * CHUNK, CHUNK), CHUNK)
        for grp in range(B_GROUPS):
            cols_u = slice(grp * B_GROUP_DIM, (grp + 1) * B_GROUP_DIM)
            cols_v = slice(B_WIDTH + grp * B_GROUP_DIM, B_WIDTH + (grp + 1) * B_GROUP_DIM)
            u = _gelu_tanh(proj_ref[rows, cols_u])
            v = _gelu_tanh(proj_ref[rows, cols_v])
            mu = jnp.mean(v, axis=-1, keepdims=True)
            vc = v - mu
            var = jnp.mean(vc * vc, axis=-1, keepdims=True)
            vn = vc * lax.rsqrt(var + EPS) * lng_ref[grp:grp + 1, :] + lnb_ref[grp:grp + 1, :]
            sv = jnp.dot(wtril_ref[grp], vn.astype(BF16), preferred_element_type=F32)
            sv = sv + bst_ref[:, grp:grp + 1]
            mix_ref[rows, cols_u] = (u * sv).astype(BF16)
        qm = proj_ref[rows, 2 * B_WIDTH:B_IN] * SCALE
        _mem_attention(qm, kkm_ref, vvm_ref, mix_ref, rows, B_WIDTH)
        return carry

    lax.fori_loop(0, n_blocks, block, 0)
    o_ref[...] = h_ref[...] + jnp.dot(mix_ref[...], wout_ref[...], preferred_element_type=F32)


def _mixer_call(name, body, h, operands, in_specs, scratch_extra, in_width):
    batch, seq, _ = h.shape
    rows = MIX_ROWS
    h_spec = pl.BlockSpec((None, rows, D_MODEL), lambda b, s: (b, s, 0))
    return pl.pallas_call(
        body,
        grid=(batch, seq // rows),
        in_specs=[h_spec] + in_specs,
        out_specs=h_spec,
        out_shape=jax.ShapeDtypeStruct(h.shape, F32),
        scratch_shapes=[pltpu.VMEM((rows, in_width), F32), pltpu.VMEM((rows, MIX_WIDTH), BF16)] + scratch_extra,
        name=name,
        compiler_params=pltpu.CompilerParams(
            dimension_semantics=("arbitrary", "arbitrary"), vmem_limit_bytes=VMEM_LIMIT),
    )(h, *operands)


def _memkv_specs():
    spec = pl.BlockSpec((None, MEM_PAIRS, 2 * N_MEM, PAIR_W), lambda b, s: (b, 0, 0, 0))
    return [spec, spec]


def _swa_mixer(h, norm_g, w_in, sinks, kkm, vvm, w_out):
    in_specs = [
        _resident((1, D_MODEL)),
        _resident((D_MODEL, A_IN)),
        pl.BlockSpec(memory_space=pltpu.SMEM),
    ] + _memkv_specs() + [_resident((MIX_WIDTH, D_MODEL))]
    scratch = [pltpu.VMEM((WINDOW, A_KV_W), F32), pltpu.VMEM((WINDOW, A_KV_W), F32)]
    return _mixer_call("swa_mixer", _swa_body, h, (norm_g.reshape(1, D_MODEL), w_in, sinks, kkm, vvm, w_out),
                       in_specs, scratch, A_IN)


def _gmlp_mixer(h, norm_g, w_in, w_s, bias_s_t, ln_g, ln_b, kkm, vvm, w_out):
    in_specs = [
        _resident((1, D_MODEL)),
        _resident((D_MODEL, B_IN)),
        _resident((B_GROUPS, CHUNK, CHUNK)),
        _resident((CHUNK, B_GROUPS)),
        _resident((B_GROUPS, B_GROUP_DIM)),
        _resident((B_GROUPS, B_GROUP_DIM)),
    ] + _memkv_specs() + [_resident((MIX_WIDTH, D_MODEL))]
    scratch = [pltpu.VMEM((B_GROUPS, CHUNK, CHUNK), BF16)]
    return _mixer_call("gmlp_mixer", _gmlp_body, h,
                       (norm_g.reshape(1, D_MODEL), w_in, w_s, bias_s_t, ln_g, ln_b, kkm, vvm, w_out),
                       in_specs, scratch, B_IN)


def _ffn_body(final_norm, h_ref, g_ref, wgu_ref, wd_ref, fg_ref, o_ref, act_ref):
    h = h_ref[...]
    hn = _rms(h, g_ref[...]).astype(BF16)
    for t in range(N_FF_TILES):
        gu = jnp.dot(hn, wgu_ref[:, 2 * t * FF_TILE:2 * (t + 1) * FF_TILE], preferred_element_type=F32)
        gate = gu[:, :FF_TILE]
        up = gu[:, FF_TILE:]
        act_ref[:, t * FF_TILE:(t + 1) * FF_TILE] = (gate / (1.0 + jnp.exp(-gate)) * up).astype(BF16)
    y = h + jnp.dot(act_ref[...], wd_ref[...], preferred_element_type=F32)
    if final_norm:
        y = _rms(y, fg_ref[...])
    o_ref[...] = y


def _ffn(h2d, norm_g, w_gu, w_down, final_g, final_norm):
    n_rows = h2d.shape[0]
    rows = FFN_ROWS
    h_spec = pl.BlockSpec((rows, D_MODEL), lambda i: (i, 0))
    return pl.pallas_call(
        functools.partial(_ffn_body, final_norm),
        grid=(n_rows // rows,),
        in_specs=[h_spec, _resident((1, D_MODEL)), _resident((D_MODEL, 2 * D_FF)),
                  _resident((D_FF, D_MODEL)), _resident((1, D_MODEL))],
        out_specs=h_spec,
        out_shape=jax.ShapeDtypeStruct(h2d.shape, F32),
        scratch_shapes=[pltpu.VMEM((rows, D_FF), BF16)],
        name="swiglu_final" if final_norm else "swiglu",
        compiler_params=pltpu.CompilerParams(
            dimension_semantics=("arbitrary",), vmem_limit_bytes=VMEM_LIMIT),
    )(h2d, norm_g.reshape(1, D_MODEL), w_gu, w_down, final_g.reshape(1, D_MODEL))


def _interleave_gate_up(w):
    d = w.shape[0]
    return w.reshape(d, 2, N_FF_TILES, FF_TILE).transpose(0, 2, 1, 3).reshape(d, 2 * D_FF)


def kernel(x, mem, mem_norm_g, mix_norm_g, ffn_norm_g, final_norm_g, a_w_in, a_sinks, a_w_out, b_w_in, b_w_s, b_bias_s, b_ln_g, b_ln_b, b_w_out, w_mem_kv, w_gate_up, w_down):
    batch, seq, _ = x.shape
    assert seq % MIX_ROWS == 0 and (batch * seq) % FFN_ROWS == 0 and MIX_ROWS % WINDOW == 0
    kkm, vvm = _mem_kv(mem, mem_norm_g, w_mem_kv.astype(BF16))
    h = x
    for i in range(DEPTH):
        j = i // 2
        if i % 2 == 0:
            h = _swa_mixer(h, mix_norm_g[i], a_w_in[j].astype(BF16), a_sinks[j], kkm[i], vvm[i],
                           a_w_out[j].astype(BF16))
        else:
            h = _gmlp_mixer(h, mix_norm_g[i], b_w_in[j].astype(BF16), b_w_s[j], b_bias_s[j].T,
                            b_ln_g[j], b_ln_b[j], kkm[i], vvm[i], b_w_out[j].astype(BF16))
        h2d = _ffn(h.reshape(batch * seq, D_MODEL), ffn_norm_g[i],
                   _interleave_gate_up(w_gate_up[i]).astype(BF16), w_down[i].astype(BF16),
                   final_norm_g, i == DEPTH - 1)
        h = h2d.reshape(batch, seq, D_MODEL)
    return h
```

```python
import functools
import math

import jax
import jax.numpy as jnp
from jax import lax
from jax.experimental import pallas as pl
from jax.experimental.pallas import tpu as pltpu

D_MODEL = 1024
DEPTH = 4
HEAD_DIM = 64
EPS = 1e-6
A_Q_HEADS = 12
A_KV_HEADS = 2
A_GROUP = A_Q_HEADS // A_KV_HEADS
WINDOW = 128
A_Q_W = A_Q_HEADS * HEAD_DIM
A_KV_W = A_KV_HEADS * HEAD_DIM
B_GROUPS = 6
B_GROUP_DIM = 128
B_WIDTH = B_GROUPS * B_GROUP_DIM
CHUNK = 128
N_MEM = 256
MEM_HEADS = 4
MEM_WIDTH = MEM_HEADS * HEAD_DIM
MIX_WIDTH = A_Q_W + MEM_WIDTH
A_IN = A_Q_W + 2 * A_KV_W + MEM_WIDTH
B_IN = 2 * B_WIDTH + MEM_WIDTH
D_FF = 2816

F32 = jnp.float32
BF16 = jnp.bfloat16

LANES = 128
PAIR_W = 2 * HEAD_DIM
VV_W = 2 * PAIR_W
MEM_PAIRS = MEM_HEADS // 2
A_PAIRS_PER_KV = A_GROUP // 2
LOG2E = math.log2(math.e)
Q_SCALE = HEAD_DIM ** -0.5 * LOG2E
FF_TILE = 256
N_FF_TILES = D_FF // FF_TILE

MIX_ROWS = 512
FFN_ROWS = 512
VMEM_LIMIT = 56 * 1024 * 1024

NT_DIMS = (((1,), (1,)), ((), ()))


def _rms(x, g):
    ms = jnp.mean(x * x, axis=-1, keepdims=True)
    return x * lax.rsqrt(ms + EPS) * g


def _lo_lanes(shape):
    assert shape[-1] == PAIR_W
    return lax.broadcasted_iota(jnp.int32, shape, len(shape) - 1) < HEAD_DIM


def _denominator_lanes(rows, first_head):
    lo = _lo_lanes((rows, PAIR_W))
    return jnp.where(lo if first_head else jnp.logical_not(lo), 1.0, 0.0).astype(BF16)


def _resident(shape):
    nd = len(shape)
    return pl.BlockSpec(shape, lambda *_: (0,) * nd, pipeline_mode=pl.Buffered(1))


def _memkv_body(mem_ref, g_ref, w_ref, kk_ref, vv_ref):
    mem_n = _rms(mem_ref[...], g_ref[...]).astype(BF16)
    kv = jnp.dot(mem_n, w_ref[...], preferred_element_type=F32)
    lo = _lo_lanes((N_MEM, PAIR_W))
    for p in range(MEM_PAIRS):
        k = kv[:, p * PAIR_W:(p + 1) * PAIR_W]
        v = kv[:, MEM_WIDTH + p * PAIR_W:MEM_WIDTH + (p + 1) * PAIR_W]
        kk_ref[p, 0:N_MEM, :] = jnp.where(lo, k, 0.0).astype(BF16)
        kk_ref[p, N_MEM:2 * N_MEM, :] = jnp.where(lo, 0.0, k).astype(BF16)
        vv_ref[p, 0:N_MEM, 0:PAIR_W] = jnp.where(lo, v, 0.0).astype(BF16)
        vv_ref[p, N_MEM:2 * N_MEM, 0:PAIR_W] = jnp.where(lo, 0.0, v).astype(BF16)
        vv_ref[p, 0:N_MEM, PAIR_W:VV_W] = _denominator_lanes(N_MEM, True)
        vv_ref[p, N_MEM:2 * N_MEM, PAIR_W:VV_W] = _denominator_lanes(N_MEM, False)


def _mem_kv(mem, mem_norm_g, w_mem_kv_bf16):
    batch = mem.shape[0]
    kk_shape = jax.ShapeDtypeStruct((DEPTH, batch, MEM_PAIRS, 2 * N_MEM, PAIR_W), BF16)
    vv_shape = jax.ShapeDtypeStruct((DEPTH, batch, MEM_PAIRS, 2 * N_MEM, VV_W), BF16)
    kk_spec = pl.BlockSpec((None, None, MEM_PAIRS, 2 * N_MEM, PAIR_W), lambda i, b: (i, b, 0, 0, 0))
    vv_spec = pl.BlockSpec((None, None, MEM_PAIRS, 2 * N_MEM, VV_W), lambda i, b: (i, b, 0, 0, 0))
    return pl.pallas_call(
        _memkv_body,
        grid=(DEPTH, batch),
        in_specs=[
            pl.BlockSpec((None, N_MEM, D_MODEL), lambda i, b: (b, 0, 0)),
            pl.BlockSpec((1, D_MODEL), lambda i, b: (0, 0)),
            pl.BlockSpec((None, D_MODEL, 2 * MEM_WIDTH), lambda i, b: (i, 0, 0)),
        ],
        out_specs=[kk_spec, vv_spec],
        out_shape=[kk_shape, vv_shape],
        name="mem_kv",
        compiler_params=pltpu.CompilerParams(dimension_semantics=("arbitrary", "arbitrary")),
    )(mem, mem_norm_g.reshape(1, D_MODEL), w_mem_kv_bf16)


def _mem_attention(qm, kkm_ref, vvm_ref, mix_ref, col0):
    for p in range(MEM_PAIRS):
        qp = qm[:, p * PAIR_W:(p + 1) * PAIR_W].astype(BF16)
        s = lax.dot_general(qp, kkm_ref[p], NT_DIMS, preferred_element_type=F32)
        es = []
        for c in range(2):
            sc = s[:, c * N_MEM:(c + 1) * N_MEM]
            m = jnp.max(sc, axis=-1, keepdims=True)
            es.append(jnp.exp2(sc - m).astype(BF16))
        o = jnp.dot(jnp.concatenate(es, axis=1), vvm_ref[p], preferred_element_type=F32)
        out = o[:, 0:PAIR_W] / o[:, PAIR_W:VV_W]
        mix_ref[:, col0 + p * PAIR_W:col0 + (p + 1) * PAIR_W] = out.astype(BF16)


W2 = 2 * WINDOW
N_CARRY = A_KV_HEADS * 4


def _swa_body(h_ref, g_ref, win_ref, sink_ref, kkm_ref, vvm_ref, wout_ref, o_ref,
              proj_ref, mix_ref, carry_ref):
    s_idx = pl.program_id(1)
    n_blocks = h_ref.shape[0] // WINDOW
    read_slot = s_idx % 2
    write_slot = 1 - read_slot

    @pl.when(s_idx == 0)
    def _():
        carry_ref[0] = jnp.zeros((N_CARRY, WINDOW, PAIR_W), BF16)

    xn = _rms(h_ref[...], g_ref[...]).astype(BF16)
    proj_ref[...] = jnp.dot(xn, win_ref[...], preferred_element_type=F32)

    qi = lax.broadcasted_iota(jnp.int32, (WINDOW, WINDOW), 0)
    kj = lax.broadcasted_iota(jnp.int32, (WINDOW, WINDOW), 1)
    from_cur = kj <= qi
    lo = _lo_lanes((WINDOW, PAIR_W))
    den_lanes = jnp.concatenate([_denominator_lanes(W2, True), _denominator_lanes(W2, False)], axis=0)
    prev_bias = jnp.where(s_idx > 0, 0.0, -jnp.inf)
    prev_tiles = [carry_ref[read_slot, t] for t in range(N_CARRY)]

    for n in range(n_blocks):
        rows = slice(n * WINDOW, (n + 1) * WINDOW)
        q = proj_ref[rows, 0:A_Q_W] * Q_SCALE
        k = proj_ref[rows, A_Q_W:A_Q_W + A_KV_W]
        v = proj_ref[rows, A_Q_W + A_KV_W:A_Q_W + 2 * A_KV_W]
        kswap = pltpu.roll(k, HEAD_DIM, 1)
        vswap = pltpu.roll(v, HEAD_DIM, 1)
        cur_tiles = []
        for kvh in range(A_KV_HEADS):
            k_first, k_second = (k, kswap) if kvh == 0 else (kswap, k)
            v_first, v_second = (v, vswap) if kvh == 0 else (vswap, v)
            cur_tiles += [jnp.where(lo, k_first, 0.0).astype(BF16), jnp.where(lo, 0.0, k_second).astype(BF16),
                          jnp.where(lo, v_first, 0.0).astype(BF16), jnp.where(lo, 0.0, v_second).astype(BF16)]

        for kvh in range(A_KV_HEADS):
            ck1, ck2, cv1, cv2 = cur_tiles[4 * kvh:4 * kvh + 4]
            pk1, pk2, pv1, pv2 = prev_tiles[4 * kvh:4 * kvh + 4]
            kk = jnp.concatenate([ck1, pk1, ck2, pk2], axis=0)
            vv = jnp.concatenate([jnp.concatenate([cv1, pv1, cv2, pv2], axis=0), den_lanes], axis=1)
            for j in range(A_PAIRS_PER_KV):
                pair = kvh * A_PAIRS_PER_KV + j
                qp = q[:, pair * PAIR_W:(pair + 1) * PAIR_W].astype(BF16)
                s = lax.dot_general(qp, kk, NT_DIMS, preferred_element_type=F32)
                e_pair, t_pair = [], []
                for c in range(2):
                    s_cur = s[:, c * W2:c * W2 + WINDOW]
                    s_prev = s[:, c * W2 + WINDOW:(c + 1) * W2]
                    if n == 0:
                        s_prev = s_prev + prev_bias
                    band = jnp.where(from_cur, s_cur, s_prev)
                    m = jnp.max(band, axis=-1, keepdims=True)
                    e = jnp.exp2(band - m).astype(BF16)
                    e_pair.append(jnp.where(from_cur, e, jnp.zeros_like(e)))
                    e_pair.append(jnp.where(from_cur, jnp.zeros_like(e), e))
                    sink = sink_ref[2 * pair + c] * LOG2E
                    t_pair.append(jnp.exp2(sink - m))
                o = jnp.dot(jnp.concatenate(e_pair, axis=1), vv, preferred_element_type=F32)
                den = o[:, PAIR_W:VV_W] + jnp.where(lo, t_pair[0], t_pair[1])
                mix_ref[rows, pair * PAIR_W:(pair + 1) * PAIR_W] = (o[:, 0:PAIR_W] / den).astype(BF16)
        prev_tiles = cur_tiles

    for t in range(N_CARRY):
        carry_ref[write_slot, t] = prev_tiles[t]

    qm = proj_ref[:, A_Q_W + 2 * A_KV_W:A_IN] * Q_SCALE
    _mem_attention(qm, kkm_ref, vvm_ref, mix_ref, A_Q_W)

    o_ref[...] = h_ref[...] + jnp.dot(mix_ref[...], wout_ref[...], preferred_element_type=F32)


def _gelu_tanh(x):
    return 0.5 * x * (1.0 + jnp.tanh(0.7978845608028654 * (x + 0.044715 * (x * x * x))))


def _gmlp_body(h_ref, g_ref, win_ref, ws_ref, bst_ref, lng_ref, lnb_ref, kkm_ref, vvm_ref, wout_ref, o_ref,
               proj_ref, mix_ref, wtril_ref):
    n_blocks = h_ref.shape[0] // CHUNK
    ti = lax.broadcasted_iota(jnp.int32, (CHUNK, CHUNK), 0)
    si = lax.broadcasted_iota(jnp.int32, (CHUNK, CHUNK), 1)
    causal = si <= ti
    for grp in range(B_GROUPS):
        wtril_ref[grp] = jnp.where(causal, ws_ref[grp], 0.0).astype(BF16)

    xn = _rms(h_ref[...], g_ref[...]).astype(BF16)
    proj_ref[...] = jnp.dot(xn, win_ref[...], preferred_element_type=F32)

    for n in range(n_blocks):
        rows = slice(n * CHUNK, (n + 1) * CHUNK)
        for grp in range(B_GROUPS):
            cols_u = slice(grp * B_GROUP_DIM, (grp + 1) * B_GROUP_DIM)
            cols_v = slice(B_WIDTH + grp * B_GROUP_DIM, B_WIDTH + (grp + 1) * B_GROUP_DIM)
            u = _gelu_tanh(proj_ref[rows, cols_u])
            v = _gelu_tanh(proj_ref[rows, cols_v])
            mu = jnp.mean(v, axis=-1, keepdims=True)
            vc = v - mu
            var = jnp.mean(vc * vc, axis=-1, keepdims=True)
            vn = vc * lax.rsqrt(var + EPS) * lng_ref[grp:grp + 1, :] + lnb_ref[grp:grp + 1, :]
            sv = jnp.dot(wtril_ref[grp], vn.astype(BF16), preferred_element_type=F32)
            sv = sv + bst_ref[:, grp:grp + 1]
            mix_ref[rows, cols_u] = (u * sv).astype(BF16)

    qm = proj_ref[:, 2 * B_WIDTH:B_IN] * Q_SCALE
    _mem_attention(qm, kkm_ref, vvm_ref, mix_ref, B_WIDTH)
    o_ref[...] = h_ref[...] + jnp.dot(mix_ref[...], wout_ref[...], preferred_element_type=F32)


def _mixer_call(name, body, h, operands, in_specs, scratch_extra, in_width):
    batch, seq, _ = h.shape
    rows = MIX_ROWS
    h_spec = pl.BlockSpec((None, rows, D_MODEL), lambda b, s: (b, s, 0))
    return pl.pallas_call(
        body,
        grid=(batch, seq // rows),
        in_specs=[h_spec] + in_specs,
        out_specs=h_spec,
        out_shape=jax.ShapeDtypeStruct(h.shape, F32),
        scratch_shapes=[pltpu.VMEM((rows, in_width), F32), pltpu.VMEM((rows, MIX_WIDTH), BF16)] + scratch_extra,
        name=name,
        compiler_params=pltpu.CompilerParams(
            dimension_semantics=("arbitrary", "arbitrary"), vmem_limit_bytes=VMEM_LIMIT),
    )(h, *operands)


def _memkv_specs():
    return [pl.BlockSpec((None, MEM_PAIRS, 2 * N_MEM, PAIR_W), lambda b, s: (b, 0, 0, 0)),
            pl.BlockSpec((None, MEM_PAIRS, 2 * N_MEM, VV_W), lambda b, s: (b, 0, 0, 0))]


def _swa_mixer(h, norm_g, w_in, sinks, kkm, vvm, w_out):
    in_specs = [
        _resident((1, D_MODEL)),
        _resident((D_MODEL, A_IN)),
        pl.BlockSpec(memory_space=pltpu.SMEM),
    ] + _memkv_specs() + [_resident((MIX_WIDTH, D_MODEL))]
    scratch = [pltpu.VMEM((2, N_CARRY, WINDOW, PAIR_W), BF16)]
    return _mixer_call("swa_mixer", _swa_body, h, (norm_g.reshape(1, D_MODEL), w_in, sinks, kkm, vvm, w_out),
                       in_specs, scratch, A_IN)


def _gmlp_mixer(h, norm_g, w_in, w_s, bias_s_t, ln_g, ln_b, kkm, vvm, w_out):
    in_specs = [
        _resident((1, D_MODEL)),
        _resident((D_MODEL, B_IN)),
        _resident((B_GROUPS, CHUNK, CHUNK)),
        _resident((CHUNK, B_GROUPS)),
        _resident((B_GROUPS, B_GROUP_DIM)),
        _resident((B_GROUPS, B_GROUP_DIM)),
    ] + _memkv_specs() + [_resident((MIX_WIDTH, D_MODEL))]
    scratch = [pltpu.VMEM((B_GROUPS, CHUNK, CHUNK), BF16)]
    return _mixer_call("gmlp_mixer", _gmlp_body, h,
                       (norm_g.reshape(1, D_MODEL), w_in, w_s, bias_s_t, ln_g, ln_b, kkm, vvm, w_out),
                       in_specs, scratch, B_IN)


def _ffn_body(final_norm, h_ref, g_ref, wgu_ref, wd_ref, fg_ref, o_ref, act_ref):
    h = h_ref[...]
    hn = _rms(h, g_ref[...]).astype(BF16)
    for t in range(N_FF_TILES):
        cols = slice(t * FF_TILE, (t + 1) * FF_TILE)
        gate = jnp.dot(hn, wgu_ref[:, cols], preferred_element_type=F32)
        up = jnp.dot(hn, wgu_ref[:, D_FF + t * FF_TILE:D_FF + (t + 1) * FF_TILE], preferred_element_type=F32)
        act_ref[:, cols] = (gate / (1.0 + jnp.exp(-gate)) * up).astype(BF16)
    y = h + jnp.dot(act_ref[...], wd_ref[...], preferred_element_type=F32)
    if final_norm:
        y = _rms(y, fg_ref[...])
    o_ref[...] = y


def _ffn(h2d, norm_g, w_gu, w_down, final_g, final_norm):
    n_rows = h2d.shape[0]
    rows = FFN_ROWS
    h_spec = pl.BlockSpec((rows, D_MODEL), lambda i: (i, 0))
    return pl.pallas_call(
        functools.partial(_ffn_body, final_norm),
        grid=(n_rows // rows,),
        in_specs=[h_spec, _resident((1, D_MODEL)), _resident((D_MODEL, 2 * D_FF)),
                  _resident((D_FF, D_MODEL)), _resident((1, D_MODEL))],
        out_specs=h_spec,
        out_shape=jax.ShapeDtypeStruct(h2d.shape, F32),
        scratch_shapes=[pltpu.VMEM((rows, D_FF), BF16)],
        name="swiglu_final" if final_norm else "swiglu",
        compiler_params=pltpu.CompilerParams(
            dimension_semantics=("arbitrary",), vmem_limit_bytes=VMEM_LIMIT),
    )(h2d, norm_g.reshape(1, D_MODEL), w_gu, w_down, final_g.reshape(1, D_MODEL))


def kernel(x, mem, mem_norm_g, mix_norm_g, ffn_norm_g, final_norm_g, a_w_in, a_sinks, a_w_out, b_w_in, b_w_s, b_bias_s, b_ln_g, b_ln_b, b_w_out, w_mem_kv, w_gate_up, w_down):
    batch, seq, _ = x.shape
    assert seq % MIX_ROWS == 0 and (batch * seq) % FFN_ROWS == 0 and MIX_ROWS % WINDOW == 0
    kkm, vvm = _mem_kv(mem, mem_norm_g, w_mem_kv.astype(BF16))
    h = x
    for i in range(DEPTH):
        j = i // 2
        if i % 2 == 0:
            h = _swa_mixer(h, mix_norm_g[i], a_w_in[j].astype(BF16), a_sinks[j], kkm[i], vvm[i],
                           a_w_out[j].astype(BF16))
        else:
            h = _gmlp_mixer(h, mix_norm_g[i], b_w_in[j].astype(BF16), b_w_s[j], b_bias_s[j].T,
                            b_ln_g[j], b_ln_b[j], kkm[i], vvm[i], b_w_out[j].astype(BF16))
        h2d = _ffn(h.reshape(batch * seq, D_MODEL), ffn_norm_g[i], w_gate_up[i].astype(BF16),
                   w_down[i].astype(BF16), final_norm_g, i == DEPTH - 1)
        h = h2d.reshape(batch, seq, D_MODEL)
    return h
```

```python
import functools
import math

import jax
import jax.numpy as jnp
from jax import lax
from jax.experimental import pallas as pl
from jax.experimental.pallas import tpu as pltpu

D_MODEL = 1024
DEPTH = 4
HEAD_DIM = 64
EPS = 1e-6
A_Q_HEADS = 12
A_KV_HEADS = 2
A_GROUP = A_Q_HEADS // A_KV_HEADS
WINDOW = 128
A_Q_W = A_Q_HEADS * HEAD_DIM
A_KV_W = A_KV_HEADS * HEAD_DIM
B_GROUPS = 6
B_GROUP_DIM = 128
B_WIDTH = B_GROUPS * B_GROUP_DIM
CHUNK = 128
N_MEM = 256
MEM_HEADS = 4
MEM_WIDTH = MEM_HEADS * HEAD_DIM
MIX_WIDTH = A_Q_W + MEM_WIDTH
A_IN = A_Q_W + 2 * A_KV_W + MEM_WIDTH
B_IN = 2 * B_WIDTH + MEM_WIDTH
D_FF = 2816

F32 = jnp.float32
BF16 = jnp.bfloat16

LANES = 128
PAIR_W = 2 * HEAD_DIM
VV_W = 2 * PAIR_W
MEM_PAIRS = MEM_HEADS // 2
A_PAIRS_PER_KV = A_GROUP // 2
LOG2E = math.log2(math.e)
Q_SCALE = HEAD_DIM ** -0.5 * LOG2E
FF_TILE = 256
N_FF_TILES = D_FF // FF_TILE

MIX_ROWS = 1024
MEM_ROWS = 256
PROJ_ROWS = 256
FFN_ROWS = 1024
VMEM_LIMIT = 56 * 1024 * 1024

NT_DIMS = (((1,), (1,)), ((), ()))


def _rms(x, g):
    ms = jnp.mean(x * x, axis=-1, keepdims=True)
    return x * lax.rsqrt(ms + EPS) * g


def _lo_lanes(shape):
    assert shape[-1] == PAIR_W
    return lax.broadcasted_iota(jnp.int32, shape, len(shape) - 1) < HEAD_DIM


def _denominator_lanes(rows, first_head):
    lo = _lo_lanes((rows, PAIR_W))
    return jnp.where(lo if first_head else jnp.logical_not(lo), 1.0, 0.0).astype(BF16)


def _resident(shape, layer=None):
    nd = len(shape)
    if layer is None:
        return pl.BlockSpec(shape, lambda *_: (0,) * nd, pipeline_mode=pl.Buffered(1))
    return pl.BlockSpec((None,) + tuple(shape), lambda *_: (layer,) + (0,) * nd, pipeline_mode=pl.Buffered(1))


def _memkv_body(mem_ref, g_ref, w_ref, kk_ref, vv_ref):
    batch = kk_ref.shape[0]
    mem_n = _rms(mem_ref[...], g_ref[...]).astype(BF16)
    kv = jnp.dot(mem_n, w_ref[...], preferred_element_type=F32)
    lo = _lo_lanes((N_MEM, PAIR_W))
    for b in range(batch):
        rows = slice(b * N_MEM, (b + 1) * N_MEM)
        for p in range(MEM_PAIRS):
            k = kv[rows, p * PAIR_W:(p + 1) * PAIR_W]
            v = kv[rows, MEM_WIDTH + p * PAIR_W:MEM_WIDTH + (p + 1) * PAIR_W]
            kk_ref[b, p, 0:N_MEM, :] = jnp.where(lo, k, 0.0).astype(BF16)
            kk_ref[b, p, N_MEM:2 * N_MEM, :] = jnp.where(lo, 0.0, k).astype(BF16)
            vv_ref[b, p, 0:N_MEM, 0:PAIR_W] = jnp.where(lo, v, 0.0).astype(BF16)
            vv_ref[b, p, N_MEM:2 * N_MEM, 0:PAIR_W] = jnp.where(lo, 0.0, v).astype(BF16)
            vv_ref[b, p, 0:N_MEM, PAIR_W:VV_W] = _denominator_lanes(N_MEM, True)
            vv_ref[b, p, N_MEM:2 * N_MEM, PAIR_W:VV_W] = _denominator_lanes(N_MEM, False)


def _mem_kv(mem, mem_norm_g, w_mem_kv_bf16):
    batch = mem.shape[0]
    kk_shape = jax.ShapeDtypeStruct((DEPTH, batch, MEM_PAIRS, 2 * N_MEM, PAIR_W), BF16)
    vv_shape = jax.ShapeDtypeStruct((DEPTH, batch, MEM_PAIRS, 2 * N_MEM, VV_W), BF16)
    kk_spec = pl.BlockSpec((None, batch, MEM_PAIRS, 2 * N_MEM, PAIR_W), lambda i: (i, 0, 0, 0, 0))
    vv_spec = pl.BlockSpec((None, batch, MEM_PAIRS, 2 * N_MEM, VV_W), lambda i: (i, 0, 0, 0, 0))
    return pl.pallas_call(
        _memkv_body,
        grid=(DEPTH,),
        in_specs=[
            pl.BlockSpec((batch * N_MEM, D_MODEL), lambda i: (0, 0)),
            pl.BlockSpec((1, D_MODEL), lambda i: (0, 0)),
            pl.BlockSpec((None, D_MODEL, 2 * MEM_WIDTH), lambda i: (i, 0, 0)),
        ],
        out_specs=[kk_spec, vv_spec],
        out_shape=[kk_shape, vv_shape],
        name="mem_kv",
        compiler_params=pltpu.CompilerParams(dimension_semantics=("arbitrary",), vmem_limit_bytes=VMEM_LIMIT),
    )(mem.reshape(batch * N_MEM, D_MODEL), mem_norm_g.reshape(1, D_MODEL), w_mem_kv_bf16)


def _in_projection(h_ref, g_ref, win_ref, proj_ref):
    for r0 in range(0, h_ref.shape[0], PROJ_ROWS):
        rows = slice(r0, r0 + PROJ_ROWS)
        xn = _rms(h_ref[rows, :], g_ref[...]).astype(BF16)
        proj_ref[rows, :] = jnp.dot(xn, win_ref[...], preferred_element_type=F32)


def _out_projection(h_ref, mix_ref, wout_ref, o_ref):
    for r0 in range(0, h_ref.shape[0], PROJ_ROWS):
        rows = slice(r0, r0 + PROJ_ROWS)
        o_ref[rows, :] = h_ref[rows, :] + jnp.dot(mix_ref[rows, :], wout_ref[...], preferred_element_type=F32)


def _mem_attention(proj_ref, q_col0, kkm_ref, vvm_ref, mix_ref, col0):
    n_rows = proj_ref.shape[0]
    for r0 in range(0, n_rows, MEM_ROWS):
        rows = slice(r0, r0 + MEM_ROWS)
        for p in range(MEM_PAIRS):
            qp = (proj_ref[rows, q_col0 + p * PAIR_W:q_col0 + (p + 1) * PAIR_W] * Q_SCALE).astype(BF16)
            s = lax.dot_general(qp, kkm_ref[p], NT_DIMS, preferred_element_type=F32)
            es = []
            for c in range(2):
                sc = s[:, c * N_MEM:(c + 1) * N_MEM]
                m = jnp.max(sc, axis=-1, keepdims=True)
                es.append(jnp.exp2(sc - m).astype(BF16))
            o = jnp.dot(jnp.concatenate(es, axis=1), vvm_ref[p], preferred_element_type=F32)
            out = o[:, 0:PAIR_W] / o[:, PAIR_W:VV_W]
            mix_ref[rows, col0 + p * PAIR_W:col0 + (p + 1) * PAIR_W] = out.astype(BF16)


W2 = 2 * WINDOW
N_CARRY = A_KV_HEADS * 4


def _swa_body(sink_row, h_ref, g_ref, win_ref, sink_ref, kkm_ref, vvm_ref, wout_ref, o_ref,
              proj_ref, mix_ref, carry_ref):
    s_idx = pl.program_id(1)
    n_blocks = h_ref.shape[0] // WINDOW
    read_slot = s_idx % 2
    write_slot = 1 - read_slot

    @pl.when(s_idx == 0)
    def _():
        carry_ref[0] = jnp.zeros((N_CARRY, WINDOW, PAIR_W), BF16)

    _in_projection(h_ref, g_ref, win_ref, proj_ref)

    qi = lax.broadcasted_iota(jnp.int32, (WINDOW, WINDOW), 0)
    kj = lax.broadcasted_iota(jnp.int32, (WINDOW, WINDOW), 1)
    from_cur = kj <= qi
    lo = _lo_lanes((WINDOW, PAIR_W))
    den_lanes = jnp.concatenate([_denominator_lanes(W2, True), _denominator_lanes(W2, False)], axis=0)
    prev_bias = jnp.where(s_idx > 0, 0.0, -jnp.inf)
    prev_tiles = [carry_ref[read_slot, t] for t in range(N_CARRY)]

    for n in range(n_blocks):
        rows = slice(n * WINDOW, (n + 1) * WINDOW)
        q = proj_ref[rows, 0:A_Q_W] * Q_SCALE
        k = proj_ref[rows, A_Q_W:A_Q_W + A_KV_W]
        v = proj_ref[rows, A_Q_W + A_KV_W:A_Q_W + 2 * A_KV_W]
        kswap = pltpu.roll(k, HEAD_DIM, 1)
        vswap = pltpu.roll(v, HEAD_DIM, 1)
        cur_tiles = []
        for kvh in range(A_KV_HEADS):
            k_first, k_second = (k, kswap) if kvh == 0 else (kswap, k)
            v_first, v_second = (v, vswap) if kvh == 0 else (vswap, v)
            cur_tiles += [jnp.where(lo, k_first, 0.0).astype(BF16), jnp.where(lo, 0.0, k_second).astype(BF16),
                          jnp.where(lo, v_first, 0.0).astype(BF16), jnp.where(lo, 0.0, v_second).astype(BF16)]

        for kvh in range(A_KV_HEADS):
            ck1, ck2, cv1, cv2 = cur_tiles[4 * kvh:4 * kvh + 4]
            pk1, pk2, pv1, pv2 = prev_tiles[4 * kvh:4 * kvh + 4]
            kk = jnp.concatenate([ck1, pk1, ck2, pk2], axis=0)
            vv = jnp.concatenate([jnp.concatenate([cv1, pv1, cv2, pv2], axis=0), den_lanes], axis=1)
            for j in range(A_PAIRS_PER_KV):
                pair = kvh * A_PAIRS_PER_KV + j
                qp = q[:, pair * PAIR_W:(pair + 1) * PAIR_W].astype(BF16)
                s = lax.dot_general(qp, kk, NT_DIMS, preferred_element_type=F32)
                e_pair, t_pair = [], []
                for c in range(2):
                    s_cur = s[:, c * W2:c * W2 + WINDOW]
                    s_prev = s[:, c * W2 + WINDOW:(c + 1) * W2]
                    if n == 0:
                        s_prev = s_prev + prev_bias
                    band = jnp.where(from_cur, s_cur, s_prev)
                    m = jnp.max(band, axis=-1, keepdims=True)
                    e = jnp.exp2(band - m).astype(BF16)
                    e_pair.append(jnp.where(from_cur, e, jnp.zeros_like(e)))
                    e_pair.append(jnp.where(from_cur, jnp.zeros_like(e), e))
                    sink = sink_ref[sink_row, 2 * pair + c] * LOG2E
                    t_pair.append(jnp.exp2(sink - m))
                o = jnp.dot(jnp.concatenate(e_pair, axis=1), vv, preferred_element_type=F32)
                den = o[:, PAIR_W:VV_W] + jnp.where(lo, t_pair[0], t_pair[1])
                mix_ref[rows, pair * PAIR_W:(pair + 1) * PAIR_W] = (o[:, 0:PAIR_W] / den).astype(BF16)
        prev_tiles = cur_tiles

    for t in range(N_CARRY):
        carry_ref[write_slot, t] = prev_tiles[t]

    _mem_attention(proj_ref, A_Q_W + 2 * A_KV_W, kkm_ref, vvm_ref, mix_ref, A_Q_W)

    _out_projection(h_ref, mix_ref, wout_ref, o_ref)


GELU_C = math.sqrt(2.0 / math.pi)


def _gelu_tanh(x):
    a = -2.0 * LOG2E * GELU_C
    t = x * (a + (a * 0.044715) * (x * x))
    return x / (1.0 + jnp.exp2(t))


def _gmlp_body(h_ref, g_ref, win_ref, ws_ref, bst_ref, lng_ref, lnb_ref, kkm_ref, vvm_ref, wout_ref, o_ref,
               proj_ref, mix_ref, wtril_ref):
    n_blocks = h_ref.shape[0] // CHUNK
    ti = lax.broadcasted_iota(jnp.int32, (CHUNK, CHUNK), 0)
    si = lax.broadcasted_iota(jnp.int32, (CHUNK, CHUNK), 1)
    causal = si <= ti
    for grp in range(B_GROUPS):
        wtril_ref[grp] = jnp.where(causal, ws_ref[grp], 0.0).astype(BF16)

    _in_projection(h_ref, g_ref, win_ref, proj_ref)

    for n in range(n_blocks):
        rows = slice(n * CHUNK, (n + 1) * CHUNK)
        for grp in range(B_GROUPS):
            cols_u = slice(grp * B_GROUP_DIM, (grp + 1) * B_GROUP_DIM)
            cols_v = slice(B_WIDTH + grp * B_GROUP_DIM, B_WIDTH + (grp + 1) * B_GROUP_DIM)
            u = _gelu_tanh(proj_ref[rows, cols_u])
            v = _gelu_tanh(proj_ref[rows, cols_v])
            mu = jnp.mean(v, axis=-1, keepdims=True)
            vc = v - mu
            var = jnp.mean(vc * vc, axis=-1, keepdims=True)
            vn = vc * lax.rsqrt(var + EPS) * lng_ref[grp:grp + 1, :] + lnb_ref[grp:grp + 1, :]
            sv = jnp.dot(wtril_ref[grp], vn.astype(BF16), preferred_element_type=F32)
            sv = sv + bst_ref[:, grp:grp + 1]
            mix_ref[rows, cols_u] = (u * sv).astype(BF16)

    _mem_attention(proj_ref, 2 * B_WIDTH, kkm_ref, vvm_ref, mix_ref, B_WIDTH)
    _out_projection(h_ref, mix_ref, wout_ref, o_ref)


def _mixer_call(name, body, h, operands, in_specs, scratch_extra, in_width):
    batch, seq, _ = h.shape
    rows = MIX_ROWS
    h_spec = pl.BlockSpec((None, rows, D_MODEL), lambda b, s: (b, s, 0))
    return pl.pallas_call(
        body,
        grid=(batch, seq // rows),
        in_specs=[h_spec] + in_specs,
        out_specs=h_spec,
        out_shape=jax.ShapeDtypeStruct(h.shape, F32),
        scratch_shapes=[pltpu.VMEM((rows, in_width), F32), pltpu.VMEM((rows, MIX_WIDTH), BF16)] + scratch_extra,
        name=name,
        compiler_params=pltpu.CompilerParams(
            dimension_semantics=("arbitrary", "arbitrary"), vmem_limit_bytes=VMEM_LIMIT),
    )(h, *operands)


def _memkv_specs(layer):
    return [pl.BlockSpec((None, None, MEM_PAIRS, 2 * N_MEM, PAIR_W), lambda b, s: (layer, b, 0, 0, 0)),
            pl.BlockSpec((None, None, MEM_PAIRS, 2 * N_MEM, VV_W), lambda b, s: (layer, b, 0, 0, 0))]


def _swa_mixer(layer, h, norm_g, w_in, sinks, kkm, vvm, w_out):
    j = layer // 2
    in_specs = [
        _resident((1, D_MODEL), layer),
        _resident((D_MODEL, A_IN), j),
        pl.BlockSpec(memory_space=pltpu.SMEM),
    ] + _memkv_specs(layer) + [_resident((MIX_WIDTH, D_MODEL), j)]
    scratch = [pltpu.VMEM((2, N_CARRY, WINDOW, PAIR_W), BF16)]
    return _mixer_call("swa_mixer", functools.partial(_swa_body, j), h, (norm_g, w_in, sinks, kkm, vvm, w_out),
                       in_specs, scratch, A_IN)


def _gmlp_mixer(layer, h, norm_g, w_in, w_s, bias_s_t, ln_g, ln_b, kkm, vvm, w_out):
    j = layer // 2
    in_specs = [
        _resident((1, D_MODEL), layer),
        _resident((D_MODEL, B_IN), j),
        _resident((B_GROUPS, CHUNK, CHUNK), j),
        _resident((CHUNK, B_GROUPS), j),
        _resident((B_GROUPS, B_GROUP_DIM), j),
        _resident((B_GROUPS, B_GROUP_DIM), j),
    ] + _memkv_specs(layer) + [_resident((MIX_WIDTH, D_MODEL), j)]
    scratch = [pltpu.VMEM((B_GROUPS, CHUNK, CHUNK), BF16)]
    return _mixer_call("gmlp_mixer", _gmlp_body, h, (norm_g, w_in, w_s, bias_s_t, ln_g, ln_b, kkm, vvm, w_out),
                       in_specs, scratch, B_IN)


def _ffn_body(final_norm, h_ref, g_ref, wgu_ref, wd_ref, fg_ref, o_ref, act_ref):
    h = h_ref[...]
    hn = _rms(h, g_ref[...]).astype(BF16)
    for t in range(N_FF_TILES):
        cols = slice(t * FF_TILE, (t + 1) * FF_TILE)
        gate = jnp.dot(hn, wgu_ref[:, cols], preferred_element_type=F32)
        up = jnp.dot(hn, wgu_ref[:, D_FF + t * FF_TILE:D_FF + (t + 1) * FF_TILE], preferred_element_type=F32)
        act_ref[:, cols] = (gate / (1.0 + jnp.exp(-gate)) * up).astype(BF16)
    y = h + jnp.dot(act_ref[...], wd_ref[...], preferred_element_type=F32)
    if final_norm:
        y = _rms(y, fg_ref[...])
    o_ref[...] = y


def _ffn(layer, h2d, norm_g, w_gu, w_down, final_g):
    final_norm = layer == DEPTH - 1
    n_rows = h2d.shape[0]
    rows = FFN_ROWS
    h_spec = pl.BlockSpec((rows, D_MODEL), lambda i: (i, 0))
    return pl.pallas_call(
        functools.partial(_ffn_body, final_norm),
        grid=(n_rows // rows,),
        in_specs=[h_spec, _resident((1, D_MODEL), layer), _resident((D_MODEL, 2 * D_FF), layer),
                  _resident((D_FF, D_MODEL), layer), _resident((1, D_MODEL))],
        out_specs=h_spec,
        out_shape=jax.ShapeDtypeStruct(h2d.shape, F32),
        scratch_shapes=[pltpu.VMEM((rows, D_FF), BF16)],
        name="swiglu_final" if final_norm else "swiglu",
        compiler_params=pltpu.CompilerParams(
            dimension_semantics=("arbitrary",), vmem_limit_bytes=VMEM_LIMIT),
    )(h2d, norm_g, w_gu, w_down, final_g)


def kernel(x, mem, mem_norm_g, mix_norm_g, ffn_norm_g, final_norm_g, a_w_in, a_sinks, a_w_out, b_w_in, b_w_s, b_bias_s, b_ln_g, b_ln_b, b_w_out, w_mem_kv, w_gate_up, w_down):
    batch, seq, _ = x.shape
    assert seq % MIX_ROWS == 0 and (batch * seq) % FFN_ROWS == 0
    assert MIX_ROWS % PROJ_ROWS == 0 and MIX_ROWS % MEM_ROWS == 0 and PROJ_ROWS % WINDOW == 0
    kkm, vvm = _mem_kv(mem, mem_norm_g, w_mem_kv.astype(BF16))
    mix_g = mix_norm_g.reshape(DEPTH, 1, D_MODEL)
    ffn_g = ffn_norm_g.reshape(DEPTH, 1, D_MODEL)
    final_g = final_norm_g.reshape(1, D_MODEL)
    a_w_in, a_w_out = a_w_in.astype(BF16), a_w_out.astype(BF16)
    b_w_in, b_w_out = b_w_in.astype(BF16), b_w_out.astype(BF16)
    w_gate_up, w_down = w_gate_up.astype(BF16), w_down.astype(BF16)
    b_bias_s_t = jnp.swapaxes(b_bias_s, 1, 2)
    h = x
    for i in range(DEPTH):
        if i % 2 == 0:
            h = _swa_mixer(i, h, mix_g, a_w_in, a_sinks, kkm, vvm, a_w_out)
        else:
            h = _gmlp_mixer(i, h, mix_g, b_w_in, b_w_s, b_bias_s_t, b_ln_g, b_ln_b, kkm, vvm, b_w_out)
        h = _ffn(i, h.reshape(batch * seq, D_MODEL), ffn_g, w_gate_up, w_down, final_g).reshape(batch, seq, D_MODEL)
    return h
```

```python
import functools
import math

import jax
import jax.numpy as jnp
from jax import lax
from jax.experimental import pallas as pl
from jax.experimental.pallas import tpu as pltpu

D_MODEL = 1024
DEPTH = 4
HEAD_DIM = 64
EPS = 1e-6
A_Q_HEADS = 12
A_KV_HEADS = 2
A_GROUP = A_Q_HEADS // A_KV_HEADS
WINDOW = 128
A_Q_W = A_Q_HEADS * HEAD_DIM
A_KV_W = A_KV_HEADS * HEAD_DIM
B_GROUPS = 6
B_GROUP_DIM = 128
B_WIDTH = B_GROUPS * B_GROUP_DIM
CHUNK = 128
N_MEM = 256
MEM_HEADS = 4
MEM_WIDTH = MEM_HEADS * HEAD_DIM
MIX_WIDTH = A_Q_W + MEM_WIDTH
A_IN = A_Q_W + 2 * A_KV_W + MEM_WIDTH
B_IN = 2 * B_WIDTH + MEM_WIDTH
D_FF = 2816

F32 = jnp.float32
BF16 = jnp.bfloat16

LANES = 128
PAIR_W = 2 * HEAD_DIM
VV_W = 2 * PAIR_W
MEM_PAIRS = MEM_HEADS // 2
A_PAIRS_PER_KV = A_GROUP // 2
LOG2E = math.log2(math.e)
Q_SCALE = HEAD_DIM ** -0.5 * LOG2E
FF_TILE = 256
N_FF_TILES = D_FF // FF_TILE

MIX_ROWS = 1024
MEM_ROWS = 256
PROJ_ROWS = 256
FFN_ROWS = 1024
VMEM_LIMIT = 56 * 1024 * 1024

NT_DIMS = (((1,), (1,)), ((), ()))


def _rms(x, g):
    ms = jnp.mean(x * x, axis=-1, keepdims=True)
    return x * lax.rsqrt(ms + EPS) * g


def _lo_lanes(shape):
    assert shape[-1] == PAIR_W
    return lax.broadcasted_iota(jnp.int32, shape, len(shape) - 1) < HEAD_DIM


def _denominator_lanes(rows, first_head):
    lo = _lo_lanes((rows, PAIR_W))
    return jnp.where(lo if first_head else jnp.logical_not(lo), 1.0, 0.0).astype(BF16)


def _resident(shape, layer=None):
    nd = len(shape)
    if layer is None:
        return pl.BlockSpec(shape, lambda *_: (0,) * nd, pipeline_mode=pl.Buffered(1))
    return pl.BlockSpec((None,) + tuple(shape), lambda *_: (layer,) + (0,) * nd, pipeline_mode=pl.Buffered(1))


def _memkv_body(mem_ref, g_ref, w_ref, kk_ref, vv_ref):
    batch = kk_ref.shape[0]
    mem_n = _rms(mem_ref[...], g_ref[...]).astype(BF16)
    kv = jnp.dot(mem_n, w_ref[...], preferred_element_type=F32)
    lo = _lo_lanes((N_MEM, PAIR_W))
    for b in range(batch):
        rows = slice(b * N_MEM, (b + 1) * N_MEM)
        for p in range(MEM_PAIRS):
            k = kv[rows, p * PAIR_W:(p + 1) * PAIR_W]
            v = kv[rows, MEM_WIDTH + p * PAIR_W:MEM_WIDTH + (p + 1) * PAIR_W]
            kk_ref[b, p, 0:N_MEM, :] = jnp.where(lo, k, 0.0).astype(BF16)
            kk_ref[b, p, N_MEM:2 * N_MEM, :] = jnp.where(lo, 0.0, k).astype(BF16)
            vv_ref[b, p, 0:N_MEM, 0:PAIR_W] = jnp.where(lo, v, 0.0).astype(BF16)
            vv_ref[b, p, N_MEM:2 * N_MEM, 0:PAIR_W] = jnp.where(lo, 0.0, v).astype(BF16)
            vv_ref[b, p, 0:N_MEM, PAIR_W:VV_W] = _denominator_lanes(N_MEM, True)
            vv_ref[b, p, N_MEM:2 * N_MEM, PAIR_W:VV_W] = _denominator_lanes(N_MEM, False)


def _mem_kv(mem, mem_norm_g, w_mem_kv_bf16):
    batch = mem.shape[0]
    kk_shape = jax.ShapeDtypeStruct((DEPTH, batch, MEM_PAIRS, 2 * N_MEM, PAIR_W), BF16)
    vv_shape = jax.ShapeDtypeStruct((DEPTH, batch, MEM_PAIRS, 2 * N_MEM, VV_W), BF16)
    kk_spec = pl.BlockSpec((None, batch, MEM_PAIRS, 2 * N_MEM, PAIR_W), lambda i: (i, 0, 0, 0, 0))
    vv_spec = pl.BlockSpec((None, batch, MEM_PAIRS, 2 * N_MEM, VV_W), lambda i: (i, 0, 0, 0, 0))
    return pl.pallas_call(
        _memkv_body,
        grid=(DEPTH,),
        in_specs=[
            pl.BlockSpec((batch * N_MEM, D_MODEL), lambda i: (0, 0)),
            pl.BlockSpec((1, D_MODEL), lambda i: (0, 0)),
            pl.BlockSpec((None, D_MODEL, 2 * MEM_WIDTH), lambda i: (i, 0, 0)),
        ],
        out_specs=[kk_spec, vv_spec],
        out_shape=[kk_shape, vv_shape],
        name="mem_kv",
        compiler_params=pltpu.CompilerParams(dimension_semantics=("arbitrary",), vmem_limit_bytes=VMEM_LIMIT),
    )(mem.reshape(batch * N_MEM, D_MODEL), mem_norm_g.reshape(1, D_MODEL), w_mem_kv_bf16)


def _in_projection(h_ref, g_ref, win_ref, proj_ref):
    for r0 in range(0, h_ref.shape[0], PROJ_ROWS):
        rows = slice(r0, r0 + PROJ_ROWS)
        xn = _rms(h_ref[rows, :], g_ref[...]).astype(BF16)
        proj_ref[rows, :] = jnp.dot(xn, win_ref[...], preferred_element_type=F32)


def _out_projection(h_ref, mix_ref, wout_ref, o_ref):
    for r0 in range(0, h_ref.shape[0], PROJ_ROWS):
        rows = slice(r0, r0 + PROJ_ROWS)
        o_ref[rows, :] = h_ref[rows, :] + jnp.dot(mix_ref[rows, :], wout_ref[...], preferred_element_type=F32)


def _mem_attention(proj_ref, q_col0, kkm_ref, vvm_ref, mix_ref, col0):
    n_rows = proj_ref.shape[0]
    for r0 in range(0, n_rows, MEM_ROWS):
        rows = slice(r0, r0 + MEM_ROWS)
        for p in range(MEM_PAIRS):
            qp = (proj_ref[rows, q_col0 + p * PAIR_W:q_col0 + (p + 1) * PAIR_W] * Q_SCALE).astype(BF16)
            s = lax.dot_general(qp, kkm_ref[p], NT_DIMS, preferred_element_type=F32)
            es = []
            for c in range(2):
                sc = s[:, c * N_MEM:(c + 1) * N_MEM]
                m = jnp.max(sc, axis=-1, keepdims=True)
                es.append(jnp.exp2(sc - m).astype(BF16))
            o = jnp.dot(jnp.concatenate(es, axis=1), vvm_ref[p], preferred_element_type=F32)
            out = o[:, 0:PAIR_W] / o[:, PAIR_W:VV_W]
            mix_ref[rows, col0 + p * PAIR_W:col0 + (p + 1) * PAIR_W] = out.astype(BF16)


W2 = 2 * WINDOW
N_CARRY = A_KV_HEADS * 4


def _swa_body(sink_row, h_ref, g_ref, win_ref, sink_ref, kkm_ref, vvm_ref, wout_ref, o_ref,
              proj_ref, mix_ref, carry_ref):
    s_idx = pl.program_id(1)
    n_blocks = h_ref.shape[0] // WINDOW
    read_slot = s_idx % 2
    write_slot = 1 - read_slot

    @pl.when(s_idx == 0)
    def _():
        carry_ref[0] = jnp.zeros((N_CARRY, WINDOW, PAIR_W), BF16)

    _in_projection(h_ref, g_ref, win_ref, proj_ref)

    qi = lax.broadcasted_iota(jnp.int32, (WINDOW, WINDOW), 0)
    kj = lax.broadcasted_iota(jnp.int32, (WINDOW, WINDOW), 1)
    from_cur = kj <= qi
    lo = _lo_lanes((WINDOW, PAIR_W))
    den_lanes = jnp.concatenate([_denominator_lanes(W2, True), _denominator_lanes(W2, False)], axis=0)
    prev_bias = jnp.where(s_idx > 0, 0.0, -jnp.inf)
    prev_tiles = [carry_ref[read_slot, t] for t in range(N_CARRY)]

    for n in range(n_blocks):
        rows = slice(n * WINDOW, (n + 1) * WINDOW)
        q = proj_ref[rows, 0:A_Q_W] * Q_SCALE
        k = proj_ref[rows, A_Q_W:A_Q_W + A_KV_W]
        v = proj_ref[rows, A_Q_W + A_KV_W:A_Q_W + 2 * A_KV_W]
        kswap = pltpu.roll(k, HEAD_DIM, 1)
        vswap = pltpu.roll(v, HEAD_DIM, 1)
        cur_tiles = []
        for kvh in range(A_KV_HEADS):
            k_first, k_second = (k, kswap) if kvh == 0 else (kswap, k)
            v_first, v_second = (v, vswap) if kvh == 0 else (vswap, v)
            cur_tiles += [jnp.where(lo, k_first, 0.0).astype(BF16), jnp.where(lo, 0.0, k_second).astype(BF16),
                          jnp.where(lo, v_first, 0.0).astype(BF16), jnp.where(lo, 0.0, v_second).astype(BF16)]

        for kvh in range(A_KV_HEADS):
            ck1, ck2, cv1, cv2 = cur_tiles[4 * kvh:4 * kvh + 4]
            pk1, pk2, pv1, pv2 = prev_tiles[4 * kvh:4 * kvh + 4]
            kk = jnp.concatenate([ck1, pk1, ck2, pk2], axis=0)
            vv = jnp.concatenate([jnp.concatenate([cv1, pv1, cv2, pv2], axis=0), den_lanes], axis=1)
            for j in range(A_PAIRS_PER_KV):
                pair = kvh * A_PAIRS_PER_KV + j
                qp = q[:, pair * PAIR_W:(pair + 1) * PAIR_W].astype(BF16)
                s = lax.dot_general(qp, kk, NT_DIMS, preferred_element_type=F32)
                e_pair, t_pair = [], []
                for c in range(2):
                    s_cur = s[:, c * W2:c * W2 + WINDOW]
                    s_prev = s[:, c * W2 + WINDOW:(c + 1) * W2]
                    if n == 0:
                        s_prev = s_prev + prev_bias
                    band = jnp.where(from_cur, s_cur, s_prev)
                    m = jnp.max(band, axis=-1, keepdims=True)
                    e = jnp.exp2(band - m).astype(BF16)
                    e_pair.append(jnp.where(from_cur, e, jnp.zeros_like(e)))
                    e_pair.append(jnp.where(from_cur, jnp.zeros_like(e), e))
                    sink = sink_ref[sink_row, 2 * pair + c] * LOG2E
                    t_pair.append(jnp.exp2(sink - m))
                o = jnp.dot(jnp.concatenate(e_pair, axis=1), vv, preferred_element_type=F32)
                den = o[:, PAIR_W:VV_W] + jnp.where(lo, t_pair[0], t_pair[1])
                mix_ref[rows, pair * PAIR_W:(pair + 1) * PAIR_W] = (o[:, 0:PAIR_W] / den).astype(BF16)
        prev_tiles = cur_tiles

    for t in range(N_CARRY):
        carry_ref[write_slot, t] = prev_tiles[t]

    _mem_attention(proj_ref, A_Q_W + 2 * A_KV_W, kkm_ref, vvm_ref, mix_ref, A_Q_W)

    _out_projection(h_ref, mix_ref, wout_ref, o_ref)


GELU_C = math.sqrt(2.0 / math.pi)


def _gelu_tanh(x):
    a = -2.0 * LOG2E * GELU_C
    t = x * (a + (a * 0.044715) * (x * x))
    return x / (1.0 + jnp.exp2(t))


def _gmlp_body(h_ref, g_ref, win_ref, ws_ref, bst_ref, lng_ref, lnb_ref, kkm_ref, vvm_ref, wout_ref, o_ref,
               proj_ref, mix_ref, wtril_ref):
    n_blocks = h_ref.shape[0] // CHUNK
    ti = lax.broadcasted_iota(jnp.int32, (CHUNK, CHUNK), 0)
    si = lax.broadcasted_iota(jnp.int32, (CHUNK, CHUNK), 1)
    causal = si <= ti
    for grp in range(B_GROUPS):
        wtril_ref[grp] = jnp.where(causal, ws_ref[grp], 0.0).astype(BF16)

    _in_projection(h_ref, g_ref, win_ref, proj_ref)

    for n in range(n_blocks):
        rows = slice(n * CHUNK, (n + 1) * CHUNK)
        for grp in range(B_GROUPS):
            cols_u = slice(grp * B_GROUP_DIM, (grp + 1) * B_GROUP_DIM)
            cols_v = slice(B_WIDTH + grp * B_GROUP_DIM, B_WIDTH + (grp + 1) * B_GROUP_DIM)
            u = _gelu_tanh(proj_ref[rows, cols_u])
            v = _gelu_tanh(proj_ref[rows, cols_v])
            mu = jnp.mean(v, axis=-1, keepdims=True)
            vc = v - mu
            var = jnp.mean(vc * vc, axis=-1, keepdims=True)
            vn = vc * lax.rsqrt(var + EPS) * lng_ref[grp:grp + 1, :] + lnb_ref[grp:grp + 1, :]
            sv = jnp.dot(wtril_ref[grp], vn.astype(BF16), preferred_element_type=F32)
            sv = sv + bst_ref[:, grp:grp + 1]
            mix_ref[rows, cols_u] = (u * sv).astype(BF16)

    _mem_attention(proj_ref, 2 * B_WIDTH, kkm_ref, vvm_ref, mix_ref, B_WIDTH)
    _out_projection(h_ref, mix_ref, wout_ref, o_ref)


def _mixer_call(name, body, h, operands, in_specs, scratch_extra, in_width):
    batch, seq, _ = h.shape
    rows = MIX_ROWS
    h_spec = pl.BlockSpec((None, rows, D_MODEL), lambda b, s: (b, s, 0))
    return pl.pallas_call(
        body,
        grid=(batch, seq // rows),
        in_specs=[h_spec] + in_specs,
        out_specs=h_spec,
        out_shape=jax.ShapeDtypeStruct(h.shape, F32),
        scratch_shapes=[pltpu.VMEM((rows, in_width), F32), pltpu.VMEM((rows, MIX_WIDTH), BF16)] + scratch_extra,
        name=name,
        compiler_params=pltpu.CompilerParams(
            dimension_semantics=("arbitrary", "arbitrary"), vmem_limit_bytes=VMEM_LIMIT),
    )(h, *operands)


def _memkv_specs(layer):
    return [pl.BlockSpec((None, None, MEM_PAIRS, 2 * N_MEM, PAIR_W), lambda b, s: (layer, b, 0, 0, 0)),
            pl.BlockSpec((None, None, MEM_PAIRS, 2 * N_MEM, VV_W), lambda b, s: (layer, b, 0, 0, 0))]


def _swa_mixer(layer, h, norm_g, w_in, sinks, kkm, vvm, w_out):
    j = layer // 2
    in_specs = [
        _resident((1, D_MODEL), layer),
        _resident((D_MODEL, A_IN), j),
        pl.BlockSpec(memory_space=pltpu.SMEM),
    ] + _memkv_specs(layer) + [_resident((MIX_WIDTH, D_MODEL), j)]
    scratch = [pltpu.VMEM((2, N_CARRY, WINDOW, PAIR_W), BF16)]
    return _mixer_call("swa_mixer", functools.partial(_swa_body, j), h, (norm_g, w_in, sinks, kkm, vvm, w_out),
                       in_specs, scratch, A_IN)


def _gmlp_mixer(layer, h, norm_g, w_in, w_s, bias_s_t, ln_g, ln_b, kkm, vvm, w_out):
    j = layer // 2
    in_specs = [
        _resident((1, D_MODEL), layer),
        _resident((D_MODEL, B_IN), j),
        _resident((B_GROUPS, CHUNK, CHUNK), j),
        _resident((CHUNK, B_GROUPS), j),
        _resident((B_GROUPS, B_GROUP_DIM), j),
        _resident((B_GROUPS, B_GROUP_DIM), j),
    ] + _memkv_specs(layer) + [_resident((MIX_WIDTH, D_MODEL), j)]
    scratch = [pltpu.VMEM((B_GROUPS, CHUNK, CHUNK), BF16)]
    return _mixer_call("gmlp_mixer", _gmlp_body, h, (norm_g, w_in, w_s, bias_s_t, ln_g, ln_b, kkm, vvm, w_out),
                       in_specs, scratch, B_IN)


def _ffn_body(final_norm, h_ref, g_ref, wgu_ref, wd_ref, fg_ref, o_ref, act_ref):
    chunks = [slice(r0, r0 + PROJ_ROWS) for r0 in range(0, h_ref.shape[0], PROJ_ROWS)]
    hn_chunks = [_rms(h_ref[rows, :], g_ref[...]).astype(BF16) for rows in chunks]
    hn = jnp.concatenate(hn_chunks, axis=0)

    def swiglu(x, t):
        cols = slice(t * FF_TILE, (t + 1) * FF_TILE)
        gate = jnp.dot(x, wgu_ref[:, cols], preferred_element_type=F32)
        up = jnp.dot(x, wgu_ref[:, D_FF + t * FF_TILE:D_FF + (t + 1) * FF_TILE], preferred_element_type=F32)
        return (gate / (1.0 + jnp.exp(-gate)) * up).astype(BF16)

    for rows, hc in zip(chunks, hn_chunks):
        act_ref[rows, 0:FF_TILE] = swiglu(hc, 0)
    for t in range(1, N_FF_TILES):
        act_ref[:, t * FF_TILE:(t + 1) * FF_TILE] = swiglu(hn, t)
    for rows in chunks:
        y = h_ref[rows, :] + jnp.dot(act_ref[rows, :], wd_ref[...], preferred_element_type=F32)
        if final_norm:
            y = _rms(y, fg_ref[...])
        o_ref[rows, :] = y


def _ffn(layer, h2d, norm_g, w_gu, w_down, final_g):
    final_norm = layer == DEPTH - 1
    n_rows = h2d.shape[0]
    rows = FFN_ROWS
    h_spec = pl.BlockSpec((rows, D_MODEL), lambda i: (i, 0))
    return pl.pallas_call(
        functools.partial(_ffn_body, final_norm),
        grid=(n_rows // rows,),
        in_specs=[h_spec, _resident((1, D_MODEL), layer), _resident((D_MODEL, 2 * D_FF), layer),
                  _resident((D_FF, D_MODEL), layer), _resident((1, D_MODEL))],
        out_specs=h_spec,
        out_shape=jax.ShapeDtypeStruct(h2d.shape, F32),
        scratch_shapes=[pltpu.VMEM((rows, D_FF), BF16)],
        name="swiglu_final" if final_norm else "swiglu",
        compiler_params=pltpu.CompilerParams(
            dimension_semantics=("arbitrary",), vmem_limit_bytes=VMEM_LIMIT),
    )(h2d, norm_g, w_gu, w_down, final_g)


def kernel(x, mem, mem_norm_g, mix_norm_g, ffn_norm_g, final_norm_g, a_w_in, a_sinks, a_w_out, b_w_in, b_w_s, b_bias_s, b_ln_g, b_ln_b, b_w_out, w_mem_kv, w_gate_up, w_down):
    batch, seq, _ = x.shape
    assert seq % MIX_ROWS == 0 and (batch * seq) % FFN_ROWS == 0
    assert MIX_ROWS % PROJ_ROWS == 0 and MIX_ROWS % MEM_ROWS == 0 and PROJ_ROWS % WINDOW == 0
    kkm, vvm = _mem_kv(mem, mem_norm_g, w_mem_kv.astype(BF16))
    mix_g = mix_norm_g.reshape(DEPTH, 1, D_MODEL)
    ffn_g = ffn_norm_g.reshape(DEPTH, 1, D_MODEL)
    final_g = final_norm_g.reshape(1, D_MODEL)
    a_w_in, a_w_out = a_w_in.astype(BF16), a_w_out.astype(BF16)
    b_w_in, b_w_out = b_w_in.astype(BF16), b_w_out.astype(BF16)
    w_gate_up, w_down = w_gate_up.astype(BF16), w_down.astype(BF16)
    b_bias_s_t = jnp.swapaxes(b_bias_s, 1, 2)
    h = x
    for i in range(DEPTH):
        if i % 2 == 0:
            h = _swa_mixer(i, h, mix_g, a_w_in, a_sinks, kkm, vvm, a_w_out)
        else:
            h = _gmlp_mixer(i, h, mix_g, b_w_in, b_w_s, b_bias_s_t, b_ln_g, b_ln_b, kkm, vvm, b_w_out)
        h = _ffn(i, h.reshape(batch * seq, D_MODEL), ffn_g, w_gate_up, w_down, final_g).reshape(batch, seq, D_MODEL)
    return h
```

```python
import functools
import math

import jax
import jax.numpy as jnp
from jax import lax
from jax.experimental import pallas as pl
from jax.experimental.pallas import tpu as pltpu

D_MODEL = 1024
DEPTH = 4
HEAD_DIM = 64
EPS = 1e-6
A_Q_HEADS = 12
A_KV_HEADS = 2
A_GROUP = A_Q_HEADS // A_KV_HEADS
WINDOW = 128
A_Q_W = A_Q_HEADS * HEAD_DIM
A_KV_W = A_KV_HEADS * HEAD_DIM
B_GROUPS = 6
B_GROUP_DIM = 128
B_WIDTH = B_GROUPS * B_GROUP_DIM
CHUNK = 128
N_MEM = 256
MEM_HEADS = 4
MEM_WIDTH = MEM_HEADS * HEAD_DIM
MIX_WIDTH = A_Q_W + MEM_WIDTH
A_IN = A_Q_W + 2 * A_KV_W + MEM_WIDTH
B_IN = 2 * B_WIDTH + MEM_WIDTH
D_FF = 2816

F32 = jnp.float32
BF16 = jnp.bfloat16

LANES = 128
PAIR_W = 2 * HEAD_DIM
VV_W = 2 * PAIR_W
MEM_PAIRS = MEM_HEADS // 2
A_PAIRS_PER_KV = A_GROUP // 2
LOG2E = math.log2(math.e)
Q_SCALE = HEAD_DIM ** -0.5 * LOG2E
FF_TILE = 256
N_FF_TILES = D_FF // FF_TILE

SWA_ROWS = 2048
GMLP_ROWS = 1024
MEM_ROWS = 256
PROJ_ROWS = 256
FFN_ROWS = 1024
VMEM_LIMIT = 56 * 1024 * 1024

NT_DIMS = (((1,), (1,)), ((), ()))


def _rms(x, g):
    ms = jnp.mean(x * x, axis=-1, keepdims=True)
    return x * lax.rsqrt(ms + EPS) * g


def _lo_lanes(shape):
    assert shape[-1] == PAIR_W
    return lax.broadcasted_iota(jnp.int32, shape, len(shape) - 1) < HEAD_DIM


def _denominator_lanes(rows, first_head):
    lo = _lo_lanes((rows, PAIR_W))
    return jnp.where(lo if first_head else jnp.logical_not(lo), 1.0, 0.0).astype(BF16)


def _resident(shape, layer=None):
    nd = len(shape)
    if layer is None:
        return pl.BlockSpec(shape, lambda *_: (0,) * nd, pipeline_mode=pl.Buffered(1))
    return pl.BlockSpec((None,) + tuple(shape), lambda *_: (layer,) + (0,) * nd, pipeline_mode=pl.Buffered(1))


def _memkv_body(mem_ref, g_ref, w_ref, kk_ref, vv_ref):
    batch = kk_ref.shape[0]
    mem_n = _rms(mem_ref[...], g_ref[...]).astype(BF16)
    kv = jnp.dot(mem_n, w_ref[...], preferred_element_type=F32)
    lo = _lo_lanes((N_MEM, PAIR_W))
    for b in range(batch):
        rows = slice(b * N_MEM, (b + 1) * N_MEM)
        for p in range(MEM_PAIRS):
            k = kv[rows, p * PAIR_W:(p + 1) * PAIR_W]
            v = kv[rows, MEM_WIDTH + p * PAIR_W:MEM_WIDTH + (p + 1) * PAIR_W]
            kk_ref[b, p, 0:N_MEM, :] = jnp.where(lo, k, 0.0).astype(BF16)
            kk_ref[b, p, N_MEM:2 * N_MEM, :] = jnp.where(lo, 0.0, k).astype(BF16)
            vv_ref[b, p, 0:N_MEM, 0:PAIR_W] = jnp.where(lo, v, 0.0).astype(BF16)
            vv_ref[b, p, N_MEM:2 * N_MEM, 0:PAIR_W] = jnp.where(lo, 0.0, v).astype(BF16)
            vv_ref[b, p, 0:N_MEM, PAIR_W:VV_W] = _denominator_lanes(N_MEM, True)
            vv_ref[b, p, N_MEM:2 * N_MEM, PAIR_W:VV_W] = _denominator_lanes(N_MEM, False)


def _mem_kv(mem, mem_norm_g, w_mem_kv_bf16):
    batch = mem.shape[0]
    kk_shape = jax.ShapeDtypeStruct((DEPTH, batch, MEM_PAIRS, 2 * N_MEM, PAIR_W), BF16)
    vv_shape = jax.ShapeDtypeStruct((DEPTH, batch, MEM_PAIRS, 2 * N_MEM, VV_W), BF16)
    kk_spec = pl.BlockSpec((None, batch, MEM_PAIRS, 2 * N_MEM, PAIR_W), lambda i: (i, 0, 0, 0, 0))
    vv_spec = pl.BlockSpec((None, batch, MEM_PAIRS, 2 * N_MEM, VV_W), lambda i: (i, 0, 0, 0, 0))
    return pl.pallas_call(
        _memkv_body,
        grid=(DEPTH,),
        in_specs=[
            pl.BlockSpec((batch * N_MEM, D_MODEL), lambda i: (0, 0)),
            pl.BlockSpec((1, D_MODEL), lambda i: (0, 0)),
            pl.BlockSpec((None, D_MODEL, 2 * MEM_WIDTH), lambda i: (i, 0, 0)),
        ],
        out_specs=[kk_spec, vv_spec],
        out_shape=[kk_shape, vv_shape],
        name="mem_kv",
        compiler_params=pltpu.CompilerParams(dimension_semantics=("arbitrary",), vmem_limit_bytes=VMEM_LIMIT),
    )(mem.reshape(batch * N_MEM, D_MODEL), mem_norm_g.reshape(1, D_MODEL), w_mem_kv_bf16)


def _in_projection(h_ref, g_ref, win_ref, proj_ref):
    for r0 in range(0, h_ref.shape[0], PROJ_ROWS):
        rows = slice(r0, r0 + PROJ_ROWS)
        xn = _rms(h_ref[rows, :], g_ref[...]).astype(BF16)
        proj_ref[rows, :] = jnp.dot(xn, win_ref[...], preferred_element_type=F32)


def _out_projection(h_ref, mix_ref, wout_ref, o_ref):
    for r0 in range(0, h_ref.shape[0], PROJ_ROWS):
        rows = slice(r0, r0 + PROJ_ROWS)
        o_ref[rows, :] = h_ref[rows, :] + jnp.dot(mix_ref[rows, :], wout_ref[...], preferred_element_type=F32)


def _mem_attention(proj_ref, q_col0, kkm_ref, vvm_ref, mix_ref, col0):
    n_rows = proj_ref.shape[0]
    for r0 in range(0, n_rows, MEM_ROWS):
        rows = slice(r0, r0 + MEM_ROWS)
        for p in range(MEM_PAIRS):
            qp = (proj_ref[rows, q_col0 + p * PAIR_W:q_col0 + (p + 1) * PAIR_W] * Q_SCALE).astype(BF16)
            s = lax.dot_general(qp, kkm_ref[p], NT_DIMS, preferred_element_type=F32)
            es = []
            for c in range(2):
                sc = s[:, c * N_MEM:(c + 1) * N_MEM]
                m = jnp.max(sc, axis=-1, keepdims=True)
                es.append(jnp.exp2(sc - m).astype(BF16))
            o = jnp.dot(jnp.concatenate(es, axis=1), vvm_ref[p], preferred_element_type=F32)
            out = o[:, 0:PAIR_W] / o[:, PAIR_W:VV_W]
            mix_ref[rows, col0 + p * PAIR_W:col0 + (p + 1) * PAIR_W] = out.astype(BF16)


W2 = 2 * WINDOW
N_CARRY = A_KV_HEADS * 4


def _swa_body(sink_row, h_ref, g_ref, win_ref, sink_ref, kkm_ref, vvm_ref, wout_ref, o_ref,
              proj_ref, mix_ref, carry_ref):
    s_idx = pl.program_id(1)
    n_blocks = h_ref.shape[0] // WINDOW
    read_slot = s_idx % 2
    write_slot = 1 - read_slot

    @pl.when(s_idx == 0)
    def _():
        carry_ref[0] = jnp.zeros((N_CARRY, WINDOW, PAIR_W), BF16)

    _in_projection(h_ref, g_ref, win_ref, proj_ref)

    qi = lax.broadcasted_iota(jnp.int32, (WINDOW, WINDOW), 0)
    kj = lax.broadcasted_iota(jnp.int32, (WINDOW, WINDOW), 1)
    from_cur = kj <= qi
    lo = _lo_lanes((WINDOW, PAIR_W))
    den_lanes = jnp.concatenate([_denominator_lanes(W2, True), _denominator_lanes(W2, False)], axis=0)
    prev_bias = jnp.where(s_idx > 0, 0.0, -jnp.inf)
    prev_tiles = [carry_ref[read_slot, t] for t in range(N_CARRY)]

    for n in range(n_blocks):
        rows = slice(n * WINDOW, (n + 1) * WINDOW)
        q = proj_ref[rows, 0:A_Q_W] * Q_SCALE
        k = proj_ref[rows, A_Q_W:A_Q_W + A_KV_W]
        v = proj_ref[rows, A_Q_W + A_KV_W:A_Q_W + 2 * A_KV_W]
        kswap = pltpu.roll(k, HEAD_DIM, 1)
        vswap = pltpu.roll(v, HEAD_DIM, 1)
        cur_tiles = []
        for kvh in range(A_KV_HEADS):
            k_first, k_second = (k, kswap) if kvh == 0 else (kswap, k)
            v_first, v_second = (v, vswap) if kvh == 0 else (vswap, v)
            cur_tiles += [jnp.where(lo, k_first, 0.0).astype(BF16), jnp.where(lo, 0.0, k_second).astype(BF16),
                          jnp.where(lo, v_first, 0.0).astype(BF16), jnp.where(lo, 0.0, v_second).astype(BF16)]

        for kvh in range(A_KV_HEADS):
            ck1, ck2, cv1, cv2 = cur_tiles[4 * kvh:4 * kvh + 4]
            pk1, pk2, pv1, pv2 = prev_tiles[4 * kvh:4 * kvh + 4]
            kk = jnp.concatenate([ck1, pk1, ck2, pk2], axis=0)
            vv = jnp.concatenate([jnp.concatenate([cv1, pv1, cv2, pv2], axis=0), den_lanes], axis=1)
            for j in range(A_PAIRS_PER_KV):
                pair = kvh * A_PAIRS_PER_KV + j
                qp = q[:, pair * PAIR_W:(pair + 1) * PAIR_W].astype(BF16)
                s = lax.dot_general(qp, kk, NT_DIMS, preferred_element_type=F32)
                e_pair, t_pair = [], []
                for c in range(2):
                    s_cur = s[:, c * W2:c * W2 + WINDOW]
                    s_prev = s[:, c * W2 + WINDOW:(c + 1) * W2]
                    if n == 0:
                        s_prev = s_prev + prev_bias
                    band = jnp.where(from_cur, s_cur, s_prev)
                    m = jnp.max(band, axis=-1, keepdims=True)
                    e = jnp.exp2(band - m).astype(BF16)
                    e_pair.append(jnp.where(from_cur, e, jnp.zeros_like(e)))
                    e_pair.append(jnp.where(from_cur, jnp.zeros_like(e), e))
                    sink = sink_ref[sink_row, 2 * pair + c] * LOG2E
                    t_pair.append(jnp.exp2(sink - m))
                o = jnp.dot(jnp.concatenate(e_pair, axis=1), vv, preferred_element_type=F32)
                den = o[:, PAIR_W:VV_W] + jnp.where(lo, t_pair[0], t_pair[1])
                mix_ref[rows, pair * PAIR_W:(pair + 1) * PAIR_W] = (o[:, 0:PAIR_W] / den).astype(BF16)
        prev_tiles = cur_tiles

    for t in range(N_CARRY):
        carry_ref[write_slot, t] = prev_tiles[t]

    _mem_attention(proj_ref, A_Q_W + 2 * A_KV_W, kkm_ref, vvm_ref, mix_ref, A_Q_W)

    _out_projection(h_ref, mix_ref, wout_ref, o_ref)


GELU_C = math.sqrt(2.0 / math.pi)


def _gelu_tanh(x):
    a = -2.0 * LOG2E * GELU_C
    t = x * (a + (a * 0.044715) * (x * x))
    return x / (1.0 + jnp.exp2(t))


def _gmlp_body(h_ref, g_ref, win_ref, ws_ref, bst_ref, lng_ref, lnb_ref, kkm_ref, vvm_ref, wout_ref, o_ref,
               proj_ref, mix_ref, wtril_ref):
    n_blocks = h_ref.shape[0] // CHUNK
    ti = lax.broadcasted_iota(jnp.int32, (CHUNK, CHUNK), 0)
    si = lax.broadcasted_iota(jnp.int32, (CHUNK, CHUNK), 1)
    causal = si <= ti
    for grp in range(B_GROUPS):
        wtril_ref[grp] = jnp.where(causal, ws_ref[grp], 0.0).astype(BF16)

    _in_projection(h_ref, g_ref, win_ref, proj_ref)

    for n in range(n_blocks):
        rows = slice(n * CHUNK, (n + 1) * CHUNK)
        for grp in range(B_GROUPS):
            cols_u = slice(grp * B_GROUP_DIM, (grp + 1) * B_GROUP_DIM)
            cols_v = slice(B_WIDTH + grp * B_GROUP_DIM, B_WIDTH + (grp + 1) * B_GROUP_DIM)
            u = _gelu_tanh(proj_ref[rows, cols_u])
            v = _gelu_tanh(proj_ref[rows, cols_v])
            mu = jnp.mean(v, axis=-1, keepdims=True)
            vc = v - mu
            var = jnp.mean(vc * vc, axis=-1, keepdims=True)
            vn = vc * lax.rsqrt(var + EPS) * lng_ref[grp:grp + 1, :] + lnb_ref[grp:grp + 1, :]
            sv = jnp.dot(wtril_ref[grp], vn.astype(BF16), preferred_element_type=F32)
            sv = sv + bst_ref[:, grp:grp + 1]
            mix_ref[rows, cols_u] = (u * sv).astype(BF16)

    _mem_attention(proj_ref, 2 * B_WIDTH, kkm_ref, vvm_ref, mix_ref, B_WIDTH)
    _out_projection(h_ref, mix_ref, wout_ref, o_ref)


def _mixer_call(name, body, rows, h, operands, in_specs, scratch_extra, in_width):
    batch, seq, _ = h.shape
    h_spec = pl.BlockSpec((None, rows, D_MODEL), lambda b, s: (b, s, 0))
    return pl.pallas_call(
        body,
        grid=(batch, seq // rows),
        in_specs=[h_spec] + in_specs,
        out_specs=h_spec,
        out_shape=jax.ShapeDtypeStruct(h.shape, F32),
        scratch_shapes=[pltpu.VMEM((rows, in_width), F32), pltpu.VMEM((rows, MIX_WIDTH), BF16)] + scratch_extra,
        name=name,
        compiler_params=pltpu.CompilerParams(
            dimension_semantics=("arbitrary", "arbitrary"), vmem_limit_bytes=VMEM_LIMIT),
    )(h, *operands)


def _memkv_specs(layer):
    return [pl.BlockSpec((None, None, MEM_PAIRS, 2 * N_MEM, PAIR_W), lambda b, s: (layer, b, 0, 0, 0)),
            pl.BlockSpec((None, None, MEM_PAIRS, 2 * N_MEM, VV_W), lambda b, s: (layer, b, 0, 0, 0))]


def _swa_mixer(layer, h, norm_g, w_in, sinks, kkm, vvm, w_out):
    j = layer // 2
    in_specs = [
        _resident((1, D_MODEL), layer),
        _resident((D_MODEL, A_IN), j),
        pl.BlockSpec(memory_space=pltpu.SMEM),
    ] + _memkv_specs(layer) + [_resident((MIX_WIDTH, D_MODEL), j)]
    scratch = [pltpu.VMEM((2, N_CARRY, WINDOW, PAIR_W), BF16)]
    return _mixer_call("swa_mixer", functools.partial(_swa_body, j), SWA_ROWS, h, (norm_g, w_in, sinks, kkm, vvm, w_out),
                       in_specs, scratch, A_IN)


def _gmlp_mixer(layer, h, norm_g, w_in, w_s, bias_s_t, ln_g, ln_b, kkm, vvm, w_out):
    j = layer // 2
    in_specs = [
        _resident((1, D_MODEL), layer),
        _resident((D_MODEL, B_IN), j),
        _resident((B_GROUPS, CHUNK, CHUNK), j),
        _resident((CHUNK, B_GROUPS), j),
        _resident((B_GROUPS, B_GROUP_DIM), j),
        _resident((B_GROUPS, B_GROUP_DIM), j),
    ] + _memkv_specs(layer) + [_resident((MIX_WIDTH, D_MODEL), j)]
    scratch = [pltpu.VMEM((B_GROUPS, CHUNK, CHUNK), BF16)]
    return _mixer_call("gmlp_mixer", _gmlp_body, GMLP_ROWS, h, (norm_g, w_in, w_s, bias_s_t, ln_g, ln_b, kkm, vvm, w_out),
                       in_specs, scratch, B_IN)


def _ffn_body(final_norm, h_ref, g_ref, wgu_ref, wd_ref, fg_ref, o_ref, act_ref):
    chunks = [slice(r0, r0 + PROJ_ROWS) for r0 in range(0, h_ref.shape[0], PROJ_ROWS)]
    hn_chunks = [_rms(h_ref[rows, :], g_ref[...]).astype(BF16) for rows in chunks]
    hn = jnp.concatenate(hn_chunks, axis=0)

    def swiglu(x, t):
        cols = slice(t * FF_TILE, (t + 1) * FF_TILE)
        gate = jnp.dot(x, wgu_ref[:, cols], preferred_element_type=F32)
        up = jnp.dot(x, wgu_ref[:, D_FF + t * FF_TILE:D_FF + (t + 1) * FF_TILE], preferred_element_type=F32)
        return (gate / (1.0 + jnp.exp(-gate)) * up).astype(BF16)

    for rows, hc in zip(chunks, hn_chunks):
        act_ref[rows, 0:FF_TILE] = swiglu(hc, 0)
    for t in range(1, N_FF_TILES):
        act_ref[:, t * FF_TILE:(t + 1) * FF_TILE] = swiglu(hn, t)
    for rows in chunks:
        y = h_ref[rows, :] + jnp.dot(act_ref[rows, :], wd_ref[...], preferred_element_type=F32)
        if final_norm:
            y = _rms(y, fg_ref[...])
        o_ref[rows, :] = y


def _ffn(layer, h2d, norm_g, w_gu, w_down, final_g):
    final_norm = layer == DEPTH - 1
    n_rows = h2d.shape[0]
    rows = FFN_ROWS
    h_spec = pl.BlockSpec((rows, D_MODEL), lambda i: (i, 0))
    return pl.pallas_call(
        functools.partial(_ffn_body, final_norm),
        grid=(n_rows // rows,),
        in_specs=[h_spec, _resident((1, D_MODEL), layer), _resident((D_MODEL, 2 * D_FF), layer),
                  _resident((D_FF, D_MODEL), layer), _resident((1, D_MODEL))],
        out_specs=h_spec,
        out_shape=jax.ShapeDtypeStruct(h2d.shape, F32),
        scratch_shapes=[pltpu.VMEM((rows, D_FF), BF16)],
        name="swiglu_final" if final_norm else "swiglu",
        compiler_params=pltpu.CompilerParams(
            dimension_semantics=("arbitrary",), vmem_limit_bytes=VMEM_LIMIT),
    )(h2d, norm_g, w_gu, w_down, final_g)


def kernel(x, mem, mem_norm_g, mix_norm_g, ffn_norm_g, final_norm_g, a_w_in, a_sinks, a_w_out, b_w_in, b_w_s, b_bias_s, b_ln_g, b_ln_b, b_w_out, w_mem_kv, w_gate_up, w_down):
    batch, seq, _ = x.shape
    assert seq % SWA_ROWS == 0 and seq % GMLP_ROWS == 0 and (batch * seq) % FFN_ROWS == 0
    assert all(r % PROJ_ROWS == 0 and r % MEM_ROWS == 0 for r in (SWA_ROWS, GMLP_ROWS, FFN_ROWS))
    assert PROJ_ROWS % WINDOW == 0 and PAIR_W == LANES
    kkm, vvm = _mem_kv(mem, mem_norm_g, w_mem_kv.astype(BF16))
    mix_g = mix_norm_g.reshape(DEPTH, 1, D_MODEL)
    ffn_g = ffn_norm_g.reshape(DEPTH, 1, D_MODEL)
    final_g = final_norm_g.reshape(1, D_MODEL)
    a_w_in, a_w_out = a_w_in.astype(BF16), a_w_out.astype(BF16)
    b_w_in, b_w_out = b_w_in.astype(BF16), b_w_out.astype(BF16)
    w_gate_up, w_down = w_gate_up.astype(BF16), w_down.astype(BF16)
    b_bias_s_t = jnp.swapaxes(b_bias_s, 1, 2)
    h = x
    for i in range(DEPTH):
        if i % 2 == 0:
            h = _swa_mixer(i, h, mix_g, a_w_in, a_sinks, kkm, vvm, a_w_out)
        else:
            h = _gmlp_mixer(i, h, mix_g, b_w_in, b_w_s, b_bias_s_t, b_ln_g, b_ln_b, kkm, vvm, b_w_out)
        h = _ffn(i, h.reshape(batch * seq, D_MODEL), ffn_g, w_gate_up, w_down, final_g).reshape(batch, seq, D_MODEL)
    return h
```

```python
import functools
import math

import jax
import jax.numpy as jnp
from jax import lax
from jax.experimental import pallas as pl
from jax.experimental.pallas import tpu as pltpu

D_MODEL = 1024
DEPTH = 4
HEAD_DIM = 64
EPS = 1e-6
A_Q_HEADS = 12
A_KV_HEADS = 2
A_GROUP = A_Q_HEADS // A_KV_HEADS
WINDOW = 128
A_Q_W = A_Q_HEADS * HEAD_DIM
A_KV_W = A_KV_HEADS * HEAD_DIM
B_GROUPS = 6
B_GROUP_DIM = 128
B_WIDTH = B_GROUPS * B_GROUP_DIM
CHUNK = 128
N_MEM = 256
MEM_HEADS = 4
MEM_WIDTH = MEM_HEADS * HEAD_DIM
MIX_WIDTH = A_Q_W + MEM_WIDTH
A_IN = A_Q_W + 2 * A_KV_W + MEM_WIDTH
B_IN = 2 * B_WIDTH + MEM_WIDTH
D_FF = 2816

F32 = jnp.float32
BF16 = jnp.bfloat16

LANES = 128
PAIR_W = 2 * HEAD_DIM
VV_W = 2 * PAIR_W
MEM_PAIRS = MEM_HEADS // 2
A_PAIRS_PER_KV = A_GROUP // 2
LOG2E = math.log2(math.e)
Q_SCALE = HEAD_DIM ** -0.5 * LOG2E
FF_TILE = 256
N_FF_TILES = D_FF // FF_TILE

SWA_ROWS = 2048
GMLP_ROWS = 1024
MEM_ROWS = 256
PROJ_ROWS = 256
FFN_ROWS = 1024
VMEM_LIMIT = 56 * 1024 * 1024

NT_DIMS = (((1,), (1,)), ((), ()))


def _rms(x, g):
    ms = jnp.mean(x * x, axis=-1, keepdims=True)
    return x * lax.rsqrt(ms + EPS) * g


def _lo_lanes(shape):
    assert shape[-1] == PAIR_W
    return lax.broadcasted_iota(jnp.int32, shape, len(shape) - 1) < HEAD_DIM


def _denominator_lanes(rows, first_head):
    lo = _lo_lanes((rows, PAIR_W))
    return jnp.where(lo if first_head else jnp.logical_not(lo), 1.0, 0.0).astype(BF16)


def _resident(shape, layer=None):
    nd = len(shape)
    if layer is None:
        return pl.BlockSpec(shape, lambda *_: (0,) * nd, pipeline_mode=pl.Buffered(1))
    return pl.BlockSpec((None,) + tuple(shape), lambda *_: (layer,) + (0,) * nd, pipeline_mode=pl.Buffered(1))


def _memkv_body(mem_ref, g_ref, w_ref, kk_ref, vv_ref):
    batch = kk_ref.shape[0]
    mem_n = _rms(mem_ref[...], g_ref[...]).astype(BF16)
    kv = jnp.dot(mem_n, w_ref[...], preferred_element_type=F32)
    lo = _lo_lanes((N_MEM, PAIR_W))
    for b in range(batch):
        rows = slice(b * N_MEM, (b + 1) * N_MEM)
        for p in range(MEM_PAIRS):
            k = kv[rows, p * PAIR_W:(p + 1) * PAIR_W]
            v = kv[rows, MEM_WIDTH + p * PAIR_W:MEM_WIDTH + (p + 1) * PAIR_W]
            kk_ref[b, p, 0:N_MEM, :] = jnp.where(lo, k, 0.0).astype(BF16)
            kk_ref[b, p, N_MEM:2 * N_MEM, :] = jnp.where(lo, 0.0, k).astype(BF16)
            vv_ref[b, p, 0:N_MEM, 0:PAIR_W] = jnp.where(lo, v, 0.0).astype(BF16)
            vv_ref[b, p, N_MEM:2 * N_MEM, 0:PAIR_W] = jnp.where(lo, 0.0, v).astype(BF16)
            vv_ref[b, p, 0:N_MEM, PAIR_W:VV_W] = _denominator_lanes(N_MEM, True)
            vv_ref[b, p, N_MEM:2 * N_MEM, PAIR_W:VV_W] = _denominator_lanes(N_MEM, False)


def _mem_kv(mem, mem_norm_g, w_mem_kv_bf16):
    batch = mem.shape[0]
    kk_shape = jax.ShapeDtypeStruct((DEPTH, batch, MEM_PAIRS, 2 * N_MEM, PAIR_W), BF16)
    vv_shape = jax.ShapeDtypeStruct((DEPTH, batch, MEM_PAIRS, 2 * N_MEM, VV_W), BF16)
    kk_spec = pl.BlockSpec((None, batch, MEM_PAIRS, 2 * N_MEM, PAIR_W), lambda i: (i, 0, 0, 0, 0))
    vv_spec = pl.BlockSpec((None, batch, MEM_PAIRS, 2 * N_MEM, VV_W), lambda i: (i, 0, 0, 0, 0))
    return pl.pallas_call(
        _memkv_body,
        grid=(DEPTH,),
        in_specs=[
            pl.BlockSpec((batch * N_MEM, D_MODEL), lambda i: (0, 0)),
            pl.BlockSpec((1, D_MODEL), lambda i: (0, 0)),
            pl.BlockSpec((None, D_MODEL, 2 * MEM_WIDTH), lambda i: (i, 0, 0)),
        ],
        out_specs=[kk_spec, vv_spec],
        out_shape=[kk_shape, vv_shape],
        name="mem_kv",
        compiler_params=pltpu.CompilerParams(dimension_semantics=("arbitrary",), vmem_limit_bytes=VMEM_LIMIT),
    )(mem.reshape(batch * N_MEM, D_MODEL), mem_norm_g.reshape(1, D_MODEL), w_mem_kv_bf16)


def _in_projection(h_ref, g_ref, win_ref, proj_ref):
    for r0 in range(0, h_ref.shape[0], PROJ_ROWS):
        rows = slice(r0, r0 + PROJ_ROWS)
        xn = _rms(h_ref[rows, :], g_ref[...]).astype(BF16)
        proj_ref[rows, :] = jnp.dot(xn, win_ref[...], preferred_element_type=F32)


def _out_projection(h_ref, mix_ref, wout_ref, o_ref):
    for r0 in range(0, h_ref.shape[0], PROJ_ROWS):
        rows = slice(r0, r0 + PROJ_ROWS)
        o_ref[rows, :] = h_ref[rows, :] + jnp.dot(mix_ref[rows, :], wout_ref[...], preferred_element_type=F32)


def _mem_attention(proj_ref, q_col0, kkm_ref, vvm_ref, mix_ref, col0):
    n_rows = proj_ref.shape[0]
    b = pl.program_id(0)
    for r0 in range(0, n_rows, MEM_ROWS):
        rows = slice(r0, r0 + MEM_ROWS)
        for p in range(MEM_PAIRS):
            qp = (proj_ref[rows, q_col0 + p * PAIR_W:q_col0 + (p + 1) * PAIR_W] * Q_SCALE).astype(BF16)
            s = lax.dot_general(qp, kkm_ref[b, p], NT_DIMS, preferred_element_type=F32)
            es = []
            for c in range(2):
                sc = s[:, c * N_MEM:(c + 1) * N_MEM]
                m = jnp.max(sc, axis=-1, keepdims=True)
                es.append(jnp.exp2(sc - m).astype(BF16))
            o = jnp.dot(jnp.concatenate(es, axis=1), vvm_ref[b, p], preferred_element_type=F32)
            out = o[:, 0:PAIR_W] / o[:, PAIR_W:VV_W]
            mix_ref[rows, col0 + p * PAIR_W:col0 + (p + 1) * PAIR_W] = out.astype(BF16)


W2 = 2 * WINDOW
N_CARRY = A_KV_HEADS * 4


def _swa_body(sink_row, h_ref, g_ref, win_ref, sink_ref, kkm_ref, vvm_ref, wout_ref, o_ref,
              proj_ref, mix_ref, carry_ref):
    s_idx = pl.program_id(1)
    n_blocks = h_ref.shape[0] // WINDOW
    read_slot = s_idx % 2
    write_slot = 1 - read_slot

    @pl.when(s_idx == 0)
    def _():
        carry_ref[0] = jnp.zeros((N_CARRY, WINDOW, PAIR_W), BF16)

    _in_projection(h_ref, g_ref, win_ref, proj_ref)

    qi = lax.broadcasted_iota(jnp.int32, (WINDOW, WINDOW), 0)
    kj = lax.broadcasted_iota(jnp.int32, (WINDOW, WINDOW), 1)
    from_cur = kj <= qi
    lo = _lo_lanes((WINDOW, PAIR_W))
    den_lanes = jnp.concatenate([_denominator_lanes(W2, True), _denominator_lanes(W2, False)], axis=0)
    prev_bias = jnp.where(s_idx > 0, 0.0, -jnp.inf)
    prev_tiles = [carry_ref[read_slot, t] for t in range(N_CARRY)]

    for n in range(n_blocks):
        rows = slice(n * WINDOW, (n + 1) * WINDOW)
        q = proj_ref[rows, 0:A_Q_W] * Q_SCALE
        k = proj_ref[rows, A_Q_W:A_Q_W + A_KV_W]
        v = proj_ref[rows, A_Q_W + A_KV_W:A_Q_W + 2 * A_KV_W]
        kswap = pltpu.roll(k, HEAD_DIM, 1)
        vswap = pltpu.roll(v, HEAD_DIM, 1)
        cur_tiles = []
        for kvh in range(A_KV_HEADS):
            k_first, k_second = (k, kswap) if kvh == 0 else (kswap, k)
            v_first, v_second = (v, vswap) if kvh == 0 else (vswap, v)
            cur_tiles += [jnp.where(lo, k_first, 0.0).astype(BF16), jnp.where(lo, 0.0, k_second).astype(BF16),
                          jnp.where(lo, v_first, 0.0).astype(BF16), jnp.where(lo, 0.0, v_second).astype(BF16)]

        for kvh in range(A_KV_HEADS):
            ck1, ck2, cv1, cv2 = cur_tiles[4 * kvh:4 * kvh + 4]
            pk1, pk2, pv1, pv2 = prev_tiles[4 * kvh:4 * kvh + 4]
            kk = jnp.concatenate([ck1, pk1, ck2, pk2], axis=0)
            vv = jnp.concatenate([jnp.concatenate([cv1, pv1, cv2, pv2], axis=0), den_lanes], axis=1)
            for j in range(A_PAIRS_PER_KV):
                pair = kvh * A_PAIRS_PER_KV + j
                qp = q[:, pair * PAIR_W:(pair + 1) * PAIR_W].astype(BF16)
                s = lax.dot_general(qp, kk, NT_DIMS, preferred_element_type=F32)
                e_pair, t_pair = [], []
                for c in range(2):
                    s_cur = s[:, c * W2:c * W2 + WINDOW]
                    s_prev = s[:, c * W2 + WINDOW:(c + 1) * W2]
                    if n == 0:
                        s_prev = s_prev + prev_bias
                    band = jnp.where(from_cur, s_cur, s_prev)
                    m = jnp.max(band, axis=-1, keepdims=True)
                    e = jnp.exp2(band - m).astype(BF16)
                    e_pair.append(jnp.where(from_cur, e, jnp.zeros_like(e)))
                    e_pair.append(jnp.where(from_cur, jnp.zeros_like(e), e))
                    sink = sink_ref[sink_row, 2 * pair + c] * LOG2E
                    t_pair.append(jnp.exp2(sink - m))
                o = jnp.dot(jnp.concatenate(e_pair, axis=1), vv, preferred_element_type=F32)
                den = o[:, PAIR_W:VV_W] + jnp.where(lo, t_pair[0], t_pair[1])
                mix_ref[rows, pair * PAIR_W:(pair + 1) * PAIR_W] = (o[:, 0:PAIR_W] / den).astype(BF16)
        prev_tiles = cur_tiles

    for t in range(N_CARRY):
        carry_ref[write_slot, t] = prev_tiles[t]

    _mem_attention(proj_ref, A_Q_W + 2 * A_KV_W, kkm_ref, vvm_ref, mix_ref, A_Q_W)

    _out_projection(h_ref, mix_ref, wout_ref, o_ref)


GELU_C = math.sqrt(2.0 / math.pi)


def _gelu_tanh(x):
    a = -2.0 * LOG2E * GELU_C
    t = x * (a + (a * 0.044715) * (x * x))
    return x / (1.0 + jnp.exp2(t))


def _gmlp_body(h_ref, g_ref, win_ref, ws_ref, bst_ref, lng_ref, lnb_ref, kkm_ref, vvm_ref, wout_ref, o_ref,
               proj_ref, mix_ref, wtril_ref):
    n_blocks = h_ref.shape[0] // CHUNK
    ti = lax.broadcasted_iota(jnp.int32, (CHUNK, CHUNK), 0)
    si = lax.broadcasted_iota(jnp.int32, (CHUNK, CHUNK), 1)
    causal = si <= ti
    for grp in range(B_GROUPS):
        wtril_ref[grp] = jnp.where(causal, ws_ref[grp], 0.0).astype(BF16)

    _in_projection(h_ref, g_ref, win_ref, proj_ref)

    for n in range(n_blocks):
        rows = slice(n * CHUNK, (n + 1) * CHUNK)
        for grp in range(B_GROUPS):
            cols_u = slice(grp * B_GROUP_DIM, (grp + 1) * B_GROUP_DIM)
            cols_v = slice(B_WIDTH + grp * B_GROUP_DIM, B_WIDTH + (grp + 1) * B_GROUP_DIM)
            u = _gelu_tanh(proj_ref[rows, cols_u])
            v = _gelu_tanh(proj_ref[rows, cols_v])
            mu = jnp.mean(v, axis=-1, keepdims=True)
            vc = v - mu
            var = jnp.mean(vc * vc, axis=-1, keepdims=True)
            vn = vc * lax.rsqrt(var + EPS) * lng_ref[grp:grp + 1, :] + lnb_ref[grp:grp + 1, :]
            sv = jnp.dot(wtril_ref[grp], vn.astype(BF16), preferred_element_type=F32)
            sv = sv + bst_ref[:, grp:grp + 1]
            mix_ref[rows, cols_u] = (u * sv).astype(BF16)

    _mem_attention(proj_ref, 2 * B_WIDTH, kkm_ref, vvm_ref, mix_ref, B_WIDTH)
    _out_projection(h_ref, mix_ref, wout_ref, o_ref)


def _mixer_call(name, body, rows, h, operands, in_specs, scratch_extra, in_width):
    batch, seq, _ = h.shape
    h_spec = pl.BlockSpec((None, rows, D_MODEL), lambda b, s: (b, s, 0))
    return pl.pallas_call(
        body,
        grid=(batch, seq // rows),
        in_specs=[h_spec] + in_specs,
        out_specs=h_spec,
        out_shape=jax.ShapeDtypeStruct(h.shape, F32),
        scratch_shapes=[pltpu.VMEM((rows, in_width), F32), pltpu.VMEM((rows, MIX_WIDTH), BF16)] + scratch_extra,
        name=name,
        compiler_params=pltpu.CompilerParams(
            dimension_semantics=("arbitrary", "arbitrary"), vmem_limit_bytes=VMEM_LIMIT),
    )(h, *operands)


def _memkv_specs(layer, batch):
    return [_resident((batch, MEM_PAIRS, 2 * N_MEM, PAIR_W), layer),
            _resident((batch, MEM_PAIRS, 2 * N_MEM, VV_W), layer)]


def _swa_mixer(layer, h, norm_g, w_in, sinks, kkm, vvm, w_out):
    j = layer // 2
    in_specs = [
        _resident((1, D_MODEL), layer),
        _resident((D_MODEL, A_IN), j),
        pl.BlockSpec(memory_space=pltpu.SMEM),
    ] + _memkv_specs(layer, h.shape[0]) + [_resident((MIX_WIDTH, D_MODEL), j)]
    scratch = [pltpu.VMEM((2, N_CARRY, WINDOW, PAIR_W), BF16)]
    return _mixer_call("swa_mixer", functools.partial(_swa_body, j), SWA_ROWS, h, (norm_g, w_in, sinks, kkm, vvm, w_out),
                       in_specs, scratch, A_IN)


def _gmlp_mixer(layer, h, norm_g, w_in, w_s, bias_s_t, ln_g, ln_b, kkm, vvm, w_out):
    j = layer // 2
    in_specs = [
        _resident((1, D_MODEL), layer),
        _resident((D_MODEL, B_IN), j),
        _resident((B_GROUPS, CHUNK, CHUNK), j),
        _resident((CHUNK, B_GROUPS), j),
        _resident((B_GROUPS, B_GROUP_DIM), j),
        _resident((B_GROUPS, B_GROUP_DIM), j),
    ] + _memkv_specs(layer, h.shape[0]) + [_resident((MIX_WIDTH, D_MODEL), j)]
    scratch = [pltpu.VMEM((B_GROUPS, CHUNK, CHUNK), BF16)]
    return _mixer_call("gmlp_mixer", _gmlp_body, GMLP_ROWS, h, (norm_g, w_in, w_s, bias_s_t, ln_g, ln_b, kkm, vvm, w_out),
                       in_specs, scratch, B_IN)


def _ffn_body(final_norm, h_ref, g_ref, wgu_ref, wd_ref, fg_ref, o_ref, act_ref):
    chunks = [slice(r0, r0 + PROJ_ROWS) for r0 in range(0, h_ref.shape[0], PROJ_ROWS)]
    hn_chunks = [_rms(h_ref[rows, :], g_ref[...]).astype(BF16) for rows in chunks]
    hn = jnp.concatenate(hn_chunks, axis=0)

    def swiglu(x, t):
        cols = slice(t * FF_TILE, (t + 1) * FF_TILE)
        gate = jnp.dot(x, wgu_ref[:, cols], preferred_element_type=F32)
        up = jnp.dot(x, wgu_ref[:, D_FF + t * FF_TILE:D_FF + (t + 1) * FF_TILE], preferred_element_type=F32)
        return (gate / (1.0 + jnp.exp(-gate)) * up).astype(BF16)

    for rows, hc in zip(chunks, hn_chunks):
        act_ref[rows, 0:FF_TILE] = swiglu(hc, 0)
    for t in range(1, N_FF_TILES):
        act_ref[:, t * FF_TILE:(t + 1) * FF_TILE] = swiglu(hn, t)
    for rows in chunks:
        y = h_ref[rows, :] + jnp.dot(act_ref[rows, :], wd_ref[...], preferred_element_type=F32)
        if final_norm:
            y = _rms(y, fg_ref[...])
        o_ref[rows, :] = y


def _ffn(layer, h2d, norm_g, w_gu, w_down, final_g):
    final_norm = layer == DEPTH - 1
    n_rows = h2d.shape[0]
    rows = FFN_ROWS
    h_spec = pl.BlockSpec((rows, D_MODEL), lambda i: (i, 0))
    return pl.pallas_call(
        functools.partial(_ffn_body, final_norm),
        grid=(n_rows // rows,),
        in_specs=[h_spec, _resident((1, D_MODEL), layer), _resident((D_MODEL, 2 * D_FF), layer),
                  _resident((D_FF, D_MODEL), layer), _resident((1, D_MODEL))],
        out_specs=h_spec,
        out_shape=jax.ShapeDtypeStruct(h2d.shape, F32),
        scratch_shapes=[pltpu.VMEM((rows, D_FF), BF16)],
        name="swiglu_final" if final_norm else "swiglu",
        compiler_params=pltpu.CompilerParams(
            dimension_semantics=("arbitrary",), vmem_limit_bytes=VMEM_LIMIT),
    )(h2d, norm_g, w_gu, w_down, final_g)


def kernel(x, mem, mem_norm_g, mix_norm_g, ffn_norm_g, final_norm_g, a_w_in, a_sinks, a_w_out, b_w_in, b_w_s, b_bias_s, b_ln_g, b_ln_b, b_w_out, w_mem_kv, w_gate_up, w_down):
    batch, seq, _ = x.shape
    assert seq % SWA_ROWS == 0 and seq % GMLP_ROWS == 0 and (batch * seq) % FFN_ROWS == 0
    assert all(r % PROJ_ROWS == 0 and r % MEM_ROWS == 0 for r in (SWA_ROWS, GMLP_ROWS, FFN_ROWS))
    assert PROJ_ROWS % WINDOW == 0 and PAIR_W == LANES
    kkm, vvm = _mem_kv(mem, mem_norm_g, w_mem_kv.astype(BF16))
    mix_g = mix_norm_g.reshape(DEPTH, 1, D_MODEL)
    ffn_g = ffn_norm_g.reshape(DEPTH, 1, D_MODEL)
    final_g = final_norm_g.reshape(1, D_MODEL)
    a_w_in, a_w_out = a_w_in.astype(BF16), a_w_out.astype(BF16)
    b_w_in, b_w_out = b_w_in.astype(BF16), b_w_out.astype(BF16)
    w_gate_up, w_down = w_gate_up.astype(BF16), w_down.astype(BF16)
    b_bias_s_t = jnp.swapaxes(b_bias_s, 1, 2)
    h = x
    for i in range(DEPTH):
        if i % 2 == 0:
            h = _swa_mixer(i, h, mix_g, a_w_in, a_sinks, kkm, vvm, a_w_out)
        else:
            h = _gmlp_mixer(i, h, mix_g, b_w_in, b_w_s, b_bias_s_t, b_ln_g, b_ln_b, kkm, vvm, b_w_out)
        h = _ffn(i, h.reshape(batch * seq, D_MODEL), ffn_g, w_gate_up, w_down, final_g).reshape(batch, seq, D_MODEL)
    return h
```

```python
import functools
import math

import jax
import jax.numpy as jnp
from jax import lax
from jax.experimental import pallas as pl
from jax.experimental.pallas import tpu as pltpu

D_MODEL = 1024
DEPTH = 4
HEAD_DIM = 64
EPS = 1e-6
A_Q_HEADS = 12
A_KV_HEADS = 2
A_GROUP = A_Q_HEADS // A_KV_HEADS
WINDOW = 128
A_Q_W = A_Q_HEADS * HEAD_DIM
A_KV_W = A_KV_HEADS * HEAD_DIM
B_GROUPS = 6
B_GROUP_DIM = 128
B_WIDTH = B_GROUPS * B_GROUP_DIM
CHUNK = 128
N_MEM = 256
MEM_HEADS = 4
MEM_WIDTH = MEM_HEADS * HEAD_DIM
MIX_WIDTH = A_Q_W + MEM_WIDTH
A_IN = A_Q_W + 2 * A_KV_W + MEM_WIDTH
B_IN = 2 * B_WIDTH + MEM_WIDTH
D_FF = 2816

F32 = jnp.float32
BF16 = jnp.bfloat16

LANES = 128
PAIR_W = 2 * HEAD_DIM
VV_W = 2 * PAIR_W
MEM_PAIRS = MEM_HEADS // 2
A_PAIRS_PER_KV = A_GROUP // 2
LOG2E = math.log2(math.e)
Q_SCALE = HEAD_DIM ** -0.5 * LOG2E
FF_TILE = 256
N_FF_TILES = D_FF // FF_TILE

SWA_ROWS = 2048
GMLP_ROWS = 1024
MEM_ROWS = 256
PROJ_ROWS = 256
FFN_ROWS = 1024
VMEM_LIMIT = 58 * 1024 * 1024

NT_DIMS = (((1,), (1,)), ((), ()))


def _rms(x, g):
    ms = jnp.mean(x * x, axis=-1, keepdims=True)
    return x * lax.rsqrt(ms + EPS) * g


def _lo_lanes(shape):
    assert shape[-1] == PAIR_W
    return lax.broadcasted_iota(jnp.int32, shape, len(shape) - 1) < HEAD_DIM


def _denominator_lanes(rows, first_head):
    lo = _lo_lanes((rows, PAIR_W))
    return jnp.where(lo if first_head else jnp.logical_not(lo), 1.0, 0.0).astype(BF16)


def _resident(shape, layer=None):
    nd = len(shape)
    if layer is None:
        return pl.BlockSpec(shape, lambda *_: (0,) * nd, pipeline_mode=pl.Buffered(1))
    return pl.BlockSpec((None,) + tuple(shape), lambda *_: (layer,) + (0,) * nd, pipeline_mode=pl.Buffered(1))


def _memkv_body(mem_ref, g_ref, w_ref, kk_ref, vv_ref):
    batch = kk_ref.shape[0]
    mem_n = _rms(mem_ref[...], g_ref[...]).astype(BF16)
    kv = jnp.dot(mem_n, w_ref[...], preferred_element_type=F32)
    lo = _lo_lanes((N_MEM, PAIR_W))
    for b in range(batch):
        rows = slice(b * N_MEM, (b + 1) * N_MEM)
        for p in range(MEM_PAIRS):
            k = kv[rows, p * PAIR_W:(p + 1) * PAIR_W]
            v = kv[rows, MEM_WIDTH + p * PAIR_W:MEM_WIDTH + (p + 1) * PAIR_W]
            kk_ref[b, p, 0:N_MEM, :] = jnp.where(lo, k, 0.0).astype(BF16)
            kk_ref[b, p, N_MEM:2 * N_MEM, :] = jnp.where(lo, 0.0, k).astype(BF16)
            vv_ref[b, p, 0:N_MEM, 0:PAIR_W] = jnp.where(lo, v, 0.0).astype(BF16)
            vv_ref[b, p, N_MEM:2 * N_MEM, 0:PAIR_W] = jnp.where(lo, 0.0, v).astype(BF16)
            vv_ref[b, p, 0:N_MEM, PAIR_W:VV_W] = _denominator_lanes(N_MEM, True)
            vv_ref[b, p, N_MEM:2 * N_MEM, PAIR_W:VV_W] = _denominator_lanes(N_MEM, False)


def _mem_kv(mem, mem_norm_g, w_mem_kv_bf16):
    batch = mem.shape[0]
    kk_shape = jax.ShapeDtypeStruct((DEPTH, batch, MEM_PAIRS, 2 * N_MEM, PAIR_W), BF16)
    vv_shape = jax.ShapeDtypeStruct((DEPTH, batch, MEM_PAIRS, 2 * N_MEM, VV_W), BF16)
    kk_spec = pl.BlockSpec((None, batch, MEM_PAIRS, 2 * N_MEM, PAIR_W), lambda i: (i, 0, 0, 0, 0))
    vv_spec = pl.BlockSpec((None, batch, MEM_PAIRS, 2 * N_MEM, VV_W), lambda i: (i, 0, 0, 0, 0))
    return pl.pallas_call(
        _memkv_body,
        grid=(DEPTH,),
        in_specs=[
            pl.BlockSpec((batch * N_MEM, D_MODEL), lambda i: (0, 0)),
            pl.BlockSpec((1, D_MODEL), lambda i: (0, 0)),
            pl.BlockSpec((None, D_MODEL, 2 * MEM_WIDTH), lambda i: (i, 0, 0)),
        ],
        out_specs=[kk_spec, vv_spec],
        out_shape=[kk_shape, vv_shape],
        name="mem_kv",
        compiler_params=pltpu.CompilerParams(dimension_semantics=("arbitrary",), vmem_limit_bytes=VMEM_LIMIT),
    )(mem.reshape(batch * N_MEM, D_MODEL), mem_norm_g.reshape(1, D_MODEL), w_mem_kv_bf16)


def _in_projection(h_ref, g_ref, win_ref, proj_ref):
    for r0 in range(0, h_ref.shape[0], PROJ_ROWS):
        rows = slice(r0, r0 + PROJ_ROWS)
        xn = _rms(h_ref[rows, :], g_ref[...]).astype(BF16)
        proj_ref[rows, :] = jnp.dot(xn, win_ref[...], preferred_element_type=F32)


def _out_projection(h_ref, mix_ref, wout_ref, fg_ref, o_ref, hn_ref):
    for r0 in range(0, h_ref.shape[0], PROJ_ROWS):
        rows = slice(r0, r0 + PROJ_ROWS)
        y = h_ref[rows, :] + jnp.dot(mix_ref[rows, :], wout_ref[...], preferred_element_type=F32)
        o_ref[rows, :] = y
        hn_ref[rows, :] = _rms(y, fg_ref[...]).astype(BF16)


def _mem_attention(proj_ref, q_col0, kkm_ref, vvm_ref, mix_ref, col0):
    n_rows = proj_ref.shape[0]
    for r0 in range(0, n_rows, MEM_ROWS):
        rows = slice(r0, r0 + MEM_ROWS)
        for p in range(MEM_PAIRS):
            qp = (proj_ref[rows, q_col0 + p * PAIR_W:q_col0 + (p + 1) * PAIR_W] * Q_SCALE).astype(BF16)
            s = lax.dot_general(qp, kkm_ref[p], NT_DIMS, preferred_element_type=F32)
            es = []
            for c in range(2):
                sc = s[:, c * N_MEM:(c + 1) * N_MEM]
                m = jnp.max(sc, axis=-1, keepdims=True)
                es.append(jnp.exp2(sc - m).astype(BF16))
            o = jnp.dot(jnp.concatenate(es, axis=1), vvm_ref[p], preferred_element_type=F32)
            out = o[:, 0:PAIR_W] / o[:, PAIR_W:VV_W]
            mix_ref[rows, col0 + p * PAIR_W:col0 + (p + 1) * PAIR_W] = out.astype(BF16)


W2 = 2 * WINDOW
N_CARRY = A_KV_HEADS * 4


def _swa_body(sink_row, h_ref, g_ref, win_ref, sink_ref, kkm_ref, vvm_ref, wout_ref, fg_ref, o_ref, hn_ref,
              proj_ref, mix_ref, carry_ref):
    s_idx = pl.program_id(1)
    n_blocks = h_ref.shape[0] // WINDOW
    read_slot = s_idx % 2
    write_slot = 1 - read_slot

    @pl.when(s_idx == 0)
    def _():
        carry_ref[0] = jnp.zeros((N_CARRY, WINDOW, PAIR_W), BF16)

    _in_projection(h_ref, g_ref, win_ref, proj_ref)

    qi = lax.broadcasted_iota(jnp.int32, (WINDOW, WINDOW), 0)
    kj = lax.broadcasted_iota(jnp.int32, (WINDOW, WINDOW), 1)
    from_cur = kj <= qi
    lo = _lo_lanes((WINDOW, PAIR_W))
    den_lanes = jnp.concatenate([_denominator_lanes(W2, True), _denominator_lanes(W2, False)], axis=0)
    prev_bias = jnp.where(s_idx > 0, 0.0, -jnp.inf)
    prev_tiles = [carry_ref[read_slot, t] for t in range(N_CARRY)]

    for n in range(n_blocks):
        rows = slice(n * WINDOW, (n + 1) * WINDOW)
        q = proj_ref[rows, 0:A_Q_W] * Q_SCALE
        k = proj_ref[rows, A_Q_W:A_Q_W + A_KV_W]
        v = proj_ref[rows, A_Q_W + A_KV_W:A_Q_W + 2 * A_KV_W]
        kswap = pltpu.roll(k, HEAD_DIM, 1)
        vswap = pltpu.roll(v, HEAD_DIM, 1)
        cur_tiles = []
        for kvh in range(A_KV_HEADS):
            k_first, k_second = (k, kswap) if kvh == 0 else (kswap, k)
            v_first, v_second = (v, vswap) if kvh == 0 else (vswap, v)
            cur_tiles += [jnp.where(lo, k_first, 0.0).astype(BF16), jnp.where(lo, 0.0, k_second).astype(BF16),
                          jnp.where(lo, v_first, 0.0).astype(BF16), jnp.where(lo, 0.0, v_second).astype(BF16)]

        for kvh in range(A_KV_HEADS):
            ck1, ck2, cv1, cv2 = cur_tiles[4 * kvh:4 * kvh + 4]
            pk1, pk2, pv1, pv2 = prev_tiles[4 * kvh:4 * kvh + 4]
            kk = jnp.concatenate([ck1, pk1, ck2, pk2], axis=0)
            vv = jnp.concatenate([jnp.concatenate([cv1, pv1, cv2, pv2], axis=0), den_lanes], axis=1)
            for j in range(A_PAIRS_PER_KV):
                pair = kvh * A_PAIRS_PER_KV + j
                qp = q[:, pair * PAIR_W:(pair + 1) * PAIR_W].astype(BF16)
                s = lax.dot_general(qp, kk, NT_DIMS, preferred_element_type=F32)
                e_pair, t_pair = [], []
                for c in range(2):
                    s_cur = s[:, c * W2:c * W2 + WINDOW]
                    s_prev = s[:, c * W2 + WINDOW:(c + 1) * W2]
                    if n == 0:
                        s_prev = s_prev + prev_bias
                    band = jnp.where(from_cur, s_cur, s_prev)
                    m = jnp.max(band, axis=-1, keepdims=True)
                    e = jnp.exp2(band - m).astype(BF16)
                    e_pair.append(jnp.where(from_cur, e, jnp.zeros_like(e)))
                    e_pair.append(jnp.where(from_cur, jnp.zeros_like(e), e))
                    sink = sink_ref[sink_row, 2 * pair + c] * LOG2E
                    t_pair.append(jnp.exp2(sink - m))
                o = jnp.dot(jnp.concatenate(e_pair, axis=1), vv, preferred_element_type=F32)
                den = o[:, PAIR_W:VV_W] + jnp.where(lo, t_pair[0], t_pair[1])
                mix_ref[rows, pair * PAIR_W:(pair + 1) * PAIR_W] = (o[:, 0:PAIR_W] / den).astype(BF16)
        prev_tiles = cur_tiles

    for t in range(N_CARRY):
        carry_ref[write_slot, t] = prev_tiles[t]

    _mem_attention(proj_ref, A_Q_W + 2 * A_KV_W, kkm_ref, vvm_ref, mix_ref, A_Q_W)

    _out_projection(h_ref, mix_ref, wout_ref, fg_ref, o_ref, hn_ref)


GELU_C = math.sqrt(2.0 / math.pi)


def _gelu_tanh(x):
    a = -2.0 * LOG2E * GELU_C
    t = x * (a + (a * 0.044715) * (x * x))
    return x / (1.0 + jnp.exp2(t))


def _gmlp_body(h_ref, g_ref, win_ref, ws_ref, bst_ref, lng_ref, lnb_ref, kkm_ref, vvm_ref, wout_ref, fg_ref,
               o_ref, hn_ref, proj_ref, mix_ref, wtril_ref):
    n_blocks = h_ref.shape[0] // CHUNK
    ti = lax.broadcasted_iota(jnp.int32, (CHUNK, CHUNK), 0)
    si = lax.broadcasted_iota(jnp.int32, (CHUNK, CHUNK), 1)
    causal = si <= ti
    for grp in range(B_GROUPS):
        wtril_ref[grp] = jnp.where(causal, ws_ref[grp], 0.0).astype(BF16)

    _in_projection(h_ref, g_ref, win_ref, proj_ref)

    for n in range(n_blocks):
        rows = slice(n * CHUNK, (n + 1) * CHUNK)
        for grp in range(B_GROUPS):
            cols_u = slice(grp * B_GROUP_DIM, (grp + 1) * B_GROUP_DIM)
            cols_v = slice(B_WIDTH + grp * B_GROUP_DIM, B_WIDTH + (grp + 1) * B_GROUP_DIM)
            u = _gelu_tanh(proj_ref[rows, cols_u])
            v = _gelu_tanh(proj_ref[rows, cols_v])
            mu = jnp.mean(v, axis=-1, keepdims=True)
            vc = v - mu
            var = jnp.mean(vc * vc, axis=-1, keepdims=True)
            vn = vc * lax.rsqrt(var + EPS) * lng_ref[grp:grp + 1, :] + lnb_ref[grp:grp + 1, :]
            sv = jnp.dot(wtril_ref[grp], vn.astype(BF16), preferred_element_type=F32)
            sv = sv + bst_ref[:, grp:grp + 1]
            mix_ref[rows, cols_u] = (u * sv).astype(BF16)

    _mem_attention(proj_ref, 2 * B_WIDTH, kkm_ref, vvm_ref, mix_ref, B_WIDTH)
    _out_projection(h_ref, mix_ref, wout_ref, fg_ref, o_ref, hn_ref)


def _mixer_call(name, body, rows, h, operands, in_specs, scratch_extra, in_width):
    batch, seq, _ = h.shape
    h_spec = pl.BlockSpec((None, rows, D_MODEL), lambda b, s: (b, s, 0))
    return pl.pallas_call(
        body,
        grid=(batch, seq // rows),
        in_specs=[h_spec] + in_specs,
        out_specs=[h_spec, h_spec],
        out_shape=[jax.ShapeDtypeStruct(h.shape, F32), jax.ShapeDtypeStruct(h.shape, BF16)],
        scratch_shapes=[pltpu.VMEM((rows, in_width), F32), pltpu.VMEM((rows, MIX_WIDTH), BF16)] + scratch_extra,
        name=name,
        compiler_params=pltpu.CompilerParams(
            dimension_semantics=("arbitrary", "arbitrary"), vmem_limit_bytes=VMEM_LIMIT),
    )(h, *operands)


def _memkv_specs(layer):
    return [pl.BlockSpec((None, None, MEM_PAIRS, 2 * N_MEM, PAIR_W), lambda b, s: (layer, b, 0, 0, 0)),
            pl.BlockSpec((None, None, MEM_PAIRS, 2 * N_MEM, VV_W), lambda b, s: (layer, b, 0, 0, 0))]


def _swa_mixer(layer, h, norm_g, w_in, sinks, kkm, vvm, w_out, ffn_g):
    j = layer // 2
    in_specs = [
        _resident((1, D_MODEL), layer),
        _resident((D_MODEL, A_IN), j),
        pl.BlockSpec(memory_space=pltpu.SMEM),
    ] + _memkv_specs(layer) + [_resident((MIX_WIDTH, D_MODEL), j), _resident((1, D_MODEL), layer)]
    scratch = [pltpu.VMEM((2, N_CARRY, WINDOW, PAIR_W), BF16)]
    return _mixer_call("swa_mixer", functools.partial(_swa_body, j), SWA_ROWS, h,
                       (norm_g, w_in, sinks, kkm, vvm, w_out, ffn_g), in_specs, scratch, A_IN)


def _gmlp_mixer(layer, h, norm_g, w_in, w_s, bias_s_t, ln_g, ln_b, kkm, vvm, w_out, ffn_g):
    j = layer // 2
    in_specs = [
        _resident((1, D_MODEL), layer),
        _resident((D_MODEL, B_IN), j),
        _resident((B_GROUPS, CHUNK, CHUNK), j),
        _resident((CHUNK, B_GROUPS), j),
        _resident((B_GROUPS, B_GROUP_DIM), j),
        _resident((B_GROUPS, B_GROUP_DIM), j),
    ] + _memkv_specs(layer) + [_resident((MIX_WIDTH, D_MODEL), j), _resident((1, D_MODEL), layer)]
    scratch = [pltpu.VMEM((B_GROUPS, CHUNK, CHUNK), BF16)]
    return _mixer_call("gmlp_mixer", _gmlp_body, GMLP_ROWS, h,
                       (norm_g, w_in, w_s, bias_s_t, ln_g, ln_b, kkm, vvm, w_out, ffn_g), in_specs, scratch, B_IN)


def _ffn_body(final_norm, h_ref, hn_ref, wgu_ref, wd_ref, fg_ref, o_ref, act_ref):
    chunks = [slice(r0, r0 + PROJ_ROWS) for r0 in range(0, h_ref.shape[0], PROJ_ROWS)]
    hn_chunks = [hn_ref[rows, :] for rows in chunks]
    hn = hn_ref[...]

    def swiglu(x, t):
        cols = slice(t * FF_TILE, (t + 1) * FF_TILE)
        gate = jnp.dot(x, wgu_ref[:, cols], preferred_element_type=F32)
        up = jnp.dot(x, wgu_ref[:, D_FF + t * FF_TILE:D_FF + (t + 1) * FF_TILE], preferred_element_type=F32)
        return (gate / (1.0 + jnp.exp(-gate)) * up).astype(BF16)

    for rows, hc in zip(chunks, hn_chunks):
        act_ref[rows, 0:FF_TILE] = swiglu(hc, 0)
    for t in range(1, N_FF_TILES):
        act_ref[:, t * FF_TILE:(t + 1) * FF_TILE] = swiglu(hn, t)
    for rows in chunks:
        y = h_ref[rows, :] + jnp.dot(act_ref[rows, :], wd_ref[...], preferred_element_type=F32)
        if final_norm:
            y = _rms(y, fg_ref[...])
        o_ref[rows, :] = y


def _ffn(layer, h2d, hn2d, w_gu, w_down, final_g):
    final_norm = layer == DEPTH - 1
    n_rows = h2d.shape[0]
    rows = FFN_ROWS
    h_spec = pl.BlockSpec((rows, D_MODEL), lambda i: (i, 0))
    return pl.pallas_call(
        functools.partial(_ffn_body, final_norm),
        grid=(n_rows // rows,),
        in_specs=[h_spec, h_spec, _resident((D_MODEL, 2 * D_FF), layer),
                  _resident((D_FF, D_MODEL), layer), _resident((1, D_MODEL))],
        out_specs=h_spec,
        out_shape=jax.ShapeDtypeStruct(h2d.shape, F32),
        scratch_shapes=[pltpu.VMEM((rows, D_FF), BF16)],
        name="swiglu_final" if final_norm else "swiglu",
        compiler_params=pltpu.CompilerParams(
            dimension_semantics=("arbitrary",), vmem_limit_bytes=VMEM_LIMIT),
    )(h2d, hn2d, w_gu, w_down, final_g)


def kernel(x, mem, mem_norm_g, mix_norm_g, ffn_norm_g, final_norm_g, a_w_in, a_sinks, a_w_out, b_w_in, b_w_s, b_bias_s, b_ln_g, b_ln_b, b_w_out, w_mem_kv, w_gate_up, w_down):
    batch, seq, _ = x.shape
    assert seq % SWA_ROWS == 0 and seq % GMLP_ROWS == 0 and (batch * seq) % FFN_ROWS == 0
    assert all(r % PROJ_ROWS == 0 and r % MEM_ROWS == 0 for r in (SWA_ROWS, GMLP_ROWS, FFN_ROWS))
    assert PROJ_ROWS % WINDOW == 0 and PAIR_W == LANES
    kkm, vvm = _mem_kv(mem, mem_norm_g, w_mem_kv.astype(BF16))
    mix_g = mix_norm_g.reshape(DEPTH, 1, D_MODEL)
    ffn_g = ffn_norm_g.reshape(DEPTH, 1, D_MODEL)
    final_g = final_norm_g.reshape(1, D_MODEL)
    a_w_in, a_w_out = a_w_in.astype(BF16), a_w_out.astype(BF16)
    b_w_in, b_w_out = b_w_in.astype(BF16), b_w_out.astype(BF16)
    w_gate_up, w_down = w_gate_up.astype(BF16), w_down.astype(BF16)
    b_bias_s_t = jnp.swapaxes(b_bias_s, 1, 2)
    h = x
    for i in range(DEPTH):
        if i % 2 == 0:
            h, hn = _swa_mixer(i, h, mix_g, a_w_in, a_sinks, kkm, vvm, a_w_out, ffn_g)
        else:
            h, hn = _gmlp_mixer(i, h, mix_g, b_w_in, b_w_s, b_bias_s_t, b_ln_g, b_ln_b, kkm, vvm, b_w_out, ffn_g)
        h = _ffn(i, h.reshape(batch * seq, D_MODEL), hn.reshape(batch * seq, D_MODEL), w_gate_up, w_down,
                 final_g).reshape(batch, seq, D_MODEL)
    return h
```
